```python
import jax, jax.numpy as jnp
from jax import lax
import numpy as np

D_MODEL = 4096
BATCH = 4
SEQ = 2048
DEPTH = 1
DEC_BATCH = 128
DEC_SEQ = 1
PAST_LEN = 16384
PAGE_SIZE = 128

HEAD_DIM = 128
MIX_WIDTH = D_MODEL
MLA_HEADS = MIX_WIDTH // (2 * HEAD_DIM)
FOX_HEADS = MIX_WIDTH // HEAD_DIM - MLA_HEADS
FOX_KV_HEADS = 2
FOX_GROUP = FOX_HEADS // FOX_KV_HEADS
Q_LORA = 1536
KV_LORA = 512
NOPE_DIM = HEAD_DIM
ROPE_DIM = 64
V_DIM = HEAD_DIM
ROPE_THETA = 10000.0
MLA_SCALE = (NOPE_DIM + ROPE_DIM) ** -0.5
FOX_SCALE = HEAD_DIM ** -0.5
D_FF = ((8 * D_MODEL + 3 * 256 - 1) // (3 * 256)) * 256
IN_SPLITS = (Q_LORA, KV_LORA, ROPE_DIM, FOX_HEADS * HEAD_DIM, FOX_KV_HEADS * HEAD_DIM, FOX_KV_HEADS * HEAD_DIM, FOX_HEADS)
D_IN = Q_LORA + KV_LORA + ROPE_DIM + FOX_HEADS * HEAD_DIM + 2 * FOX_KV_HEADS * HEAD_DIM + FOX_HEADS
N_MOD = 6
Q_BLOCK = 128
EPS = 1e-6

kernel_name = 'hybrid_mla_fox_adaln_decode_step'


def rmsnorm(x, g):
    xf = x.astype(jnp.float32)
    y = xf * lax.rsqrt(jnp.mean(xf * xf, axis=-1, keepdims=True) + EPS)
    return (y * g.astype(jnp.float32)).astype(x.dtype)


def rope(x, pos):
    half = ROPE_DIM // 2
    inv_freq = jnp.power(ROPE_THETA, -jnp.arange(half, dtype=jnp.float32) * 2.0 / ROPE_DIM)
    ang = pos.astype(jnp.float32)[:, None] * inv_freq[None, :]
    cos = jnp.cos(ang)[None, :, None, :]
    sin = jnp.sin(ang)[None, :, None, :]
    xf = x.astype(jnp.float32)
    x1, x2 = xf[..., :half], xf[..., half:]
    return jnp.concatenate([x1 * cos - x2 * sin, x2 * cos + x1 * sin], axis=-1).astype(x.dtype)


def causal_mask(q_off, n_q, n_k):
    return (q_off + jnp.arange(n_q))[:, None] >= jnp.arange(n_k)[None, :]


def suffix_logf(logf):
    lf = logf.astype(jnp.float32)
    return lax.cumsum(lf, axis=1, reverse=True) - lf


def modulation(c, w_ada, b_ada):
    mod = (jax.nn.silu(c) @ w_ada + b_ada).reshape(c.shape[0], N_MOD, 1, D_MODEL)
    return [mod[:, i] for i in range(N_MOD)]


def project_mixers(h, pos, w_in, g_q, w_q_up, g_kv, w_uk, b_f):
    b, t, _ = h.shape
    z = h @ w_in
    cuts = np.cumsum(IN_SPLITS)[:-1].tolist()
    cq, ckv, kr, fq, fk, fv, ff = jnp.split(z, cuts, axis=-1)
    q = jnp.einsum('btc,chk->bthk', rmsnorm(cq, g_q), w_q_up)
    q_rope = rope(q[..., NOPE_DIM:], pos)
    q_lat = jnp.einsum('bthn,chn->bthc', q[..., :NOPE_DIM], w_uk)
    c_kv = rmsnorm(ckv, g_kv)
    k_rope = rope(kr[:, :, None, :], pos)[:, :, 0, :]
    fq = fq.reshape(b, t, FOX_HEADS, HEAD_DIM)
    fk = fk.reshape(b, t, FOX_KV_HEADS, HEAD_DIM)
    fv = fv.reshape(b, t, FOX_KV_HEADS, HEAD_DIM)
    logf = jax.nn.log_sigmoid((ff + b_f).astype(jnp.float32))
    return (q_lat, q_rope, c_kv, k_rope, fq, fk, fv, logf)


def mla_attend(q_lat, q_rope, k_lat, k_rope, mask, w_uv):
    s = (jnp.einsum('bqhc,bkc->bhqk', q_lat, k_lat, preferred_element_type=jnp.float32)
         + jnp.einsum('bqhr,bkr->bhqk', q_rope, k_rope, preferred_element_type=jnp.float32)) * MLA_SCALE
    p = jax.nn.softmax(jnp.where(mask, s, -jnp.inf), axis=-1).astype(k_lat.dtype)
    o_lat = jnp.einsum('bhqk,bkc->bqhc', p, k_lat)
    o = jnp.einsum('bqhc,chv->bqhv', o_lat, w_uv)
    return o.reshape(o.shape[0], o.shape[1], MLA_HEADS * V_DIM)


def fox_attend(q, k, v, e_q, e_k, mask):
    b, tq = q.shape[0], q.shape[1]
    tk = k.shape[1]
    qg = q.reshape(b, tq, FOX_KV_HEADS, FOX_GROUP, HEAD_DIM)
    s = jnp.einsum('bqngd,bknd->bngqk', qg, k, preferred_element_type=jnp.float32) * FOX_SCALE
    bk = e_k.reshape(b, tk, FOX_KV_HEADS, FOX_GROUP).transpose(0, 2, 3, 1)[:, :, :, None, :]
    bq = e_q.reshape(b, tq, FOX_KV_HEADS, FOX_GROUP).transpose(0, 2, 3, 1)[:, :, :, :, None]
    s = s + bk - bq
    p = jax.nn.softmax(jnp.where(mask, s, -jnp.inf), axis=-1).astype(v.dtype)
    o = jnp.einsum('bngqk,bknd->bqngd', p, v)
    return o.reshape(b, tq, FOX_HEADS * HEAD_DIM)


def prompt_mixers(pr, w_uv):
    q_lat, q_rope, c_kv, k_rope, fq, fk, fv, logf = pr
    t = q_lat.shape[1]
    e = suffix_logf(logf)
    ym, yf = [], []
    for i in range(t // Q_BLOCK):
        q0, q1 = i * Q_BLOCK, (i + 1) * Q_BLOCK
        mask = causal_mask(q0, Q_BLOCK, q1)
        ym.append(mla_attend(q_lat[:, q0:q1], q_rope[:, q0:q1], c_kv[:, :q1], k_rope[:, :q1], mask, w_uv))
        yf.append(fox_attend(fq[:, q0:q1], fk[:, :q1], fv[:, :q1], e[:, q0:q1], e[:, :q1], mask))
    return jnp.concatenate(ym, axis=1), jnp.concatenate(yf, axis=1)


def gather_past(cache, page_table):
    g = jnp.take(cache, page_table, axis=0)
    return g.reshape((page_table.shape[0], page_table.shape[1] * PAGE_SIZE) + cache.shape[2:])


def sample_mixers(pr, w_uv, lat_c, rope_c, k_c, v_c, logf_c, page_table):
    q_lat, q_rope, c_kv, k_rope, fq, fk, fv, logf = pr
    t = q_lat.shape[1]
    past = page_table.shape[1] * PAGE_SIZE
    lat_all = jnp.concatenate([gather_past(lat_c, page_table), c_kv], axis=1)
    rope_all = jnp.concatenate([gather_past(rope_c, page_table), k_rope], axis=1)
    k_all = jnp.concatenate([gather_past(k_c, page_table), fk], axis=1)
    v_all = jnp.concatenate([gather_past(v_c, page_table), fv], axis=1)
    logf_all = jnp.concatenate([gather_past(logf_c, page_table).astype(jnp.float32), logf], axis=1)
    e = suffix_logf(logf_all)
    mask = causal_mask(past, t, past + t)
    ym = mla_attend(q_lat, q_rope, lat_all, rope_all, mask, w_uv)
    yf = fox_attend(fq, k_all, v_all, e[:, past:], e, mask)
    return ym, yf


def mix_output(ym, yf, g_mla, g_fox, w_out):
    y = jnp.concatenate([rmsnorm(ym, g_mla), rmsnorm(yf, g_fox)], axis=-1)
    return y @ w_out


def swiglu(h, w_gate, w_up, w_down):
    return (jax.nn.silu(h @ w_gate) * (h @ w_up)) @ w_down


def setup_inputs(seed: int = 0) -> dict:
    key = jax.random.key(seed)
    ks = jax.random.split(key, 28)
    f32 = jnp.float32
    n_pages = PAST_LEN // PAGE_SIZE
    n_used = DEC_BATCH * n_pages
    n_phys = n_used + (n_used + 3) // 4

    def nrm(k, shape, scale):
        return jax.random.normal(k, shape, f32) * scale

    def gain(k, shape):
        return 1.0 + 0.02 * jax.random.normal(k, shape, f32)

    return {
        'x_prompt': nrm(ks[0], (BATCH, SEQ, D_MODEL), 1.0),
        'x_sample': nrm(ks[1], (DEC_BATCH, DEC_SEQ, D_MODEL), 1.0),
        'c_prompt': nrm(ks[2], (BATCH, D_MODEL), 1.0),
        'c_sample': nrm(ks[3], (DEC_BATCH, D_MODEL), 1.0),
        'cache_mla_latent': nrm(ks[4], (DEPTH, n_phys, PAGE_SIZE, KV_LORA), 1.0),
        'cache_mla_krope': nrm(ks[5], (DEPTH, n_phys, PAGE_SIZE, ROPE_DIM), 1.0),
        'cache_fox_k': nrm(ks[6], (DEPTH, n_phys, PAGE_SIZE, FOX_KV_HEADS, HEAD_DIM), 1.0),
        'cache_fox_v': nrm(ks[7], (DEPTH, n_phys, PAGE_SIZE, FOX_KV_HEADS, HEAD_DIM), 1.0),
        'cache_fox_logf': jax.nn.log_sigmoid(3.5 + jax.random.normal(ks[8], (DEPTH, n_phys, PAGE_SIZE, FOX_HEADS), f32)),
        'page_table': jax.random.permutation(ks[9], n_phys)[:n_used].reshape(DEC_BATCH, n_pages).astype(jnp.int32),
        'w_ada': nrm(ks[10], (DEPTH, D_MODEL, N_MOD * D_MODEL), 0.5 * D_MODEL ** -0.5),
        'b_ada': nrm(ks[11], (DEPTH, N_MOD * D_MODEL), 0.01),
        'g_attn_norm': gain(ks[12], (DEPTH, D_MODEL)),
        'w_in': nrm(ks[13], (DEPTH, D_MODEL, D_IN), D_MODEL ** -0.5),
        'g_q_norm': gain(ks[14], (DEPTH, Q_LORA)),
        'w_q_up': nrm(ks[15], (DEPTH, Q_LORA, MLA_HEADS, NOPE_DIM + ROPE_DIM), Q_LORA ** -0.5),
        'g_kv_norm': gain(ks[16], (DEPTH, KV_LORA)),
        'w_uk': nrm(ks[17], (DEPTH, KV_LORA, MLA_HEADS, NOPE_DIM), KV_LORA ** -0.5),
        'w_uv': nrm(ks[18], (DEPTH, KV_LORA, MLA_HEADS, V_DIM), KV_LORA ** -0.5),
        'b_forget': jax.random.uniform(ks[19], (DEPTH, FOX_HEADS), f32, 1.0, 6.0),
        'g_mla_out': gain(ks[20], (DEPTH, MLA_HEADS * V_DIM)),
        'g_fox_out': gain(ks[21], (DEPTH, FOX_HEADS * HEAD_DIM)),
        'w_out': nrm(ks[22], (DEPTH, MIX_WIDTH, D_MODEL), MIX_WIDTH ** -0.5),
        'g_ffn_norm': gain(ks[23], (DEPTH, D_MODEL)),
        'w_gate': nrm(ks[24], (DEPTH, D_MODEL, D_FF), D_MODEL ** -0.5),
        'w_up': nrm(ks[25], (DEPTH, D_MODEL, D_FF), D_MODEL ** -0.5),
        'w_down': nrm(ks[26], (DEPTH, D_FF, D_MODEL), D_FF ** -0.5),
        'g_final': gain(ks[27], (D_MODEL,)),
    }


def reference(x_prompt, x_sample, c_prompt, c_sample, cache_mla_latent, cache_mla_krope, cache_fox_k,
              cache_fox_v, cache_fox_logf, page_table, w_ada, b_ada, g_attn_norm, w_in, g_q_norm, w_q_up,
              g_kv_norm, w_uk, w_uv, b_forget, g_mla_out, g_fox_out, w_out, g_ffn_norm, w_gate, w_up,
              w_down, g_final):
    past = page_table.shape[1] * PAGE_SIZE
    pos_p = jnp.arange(x_prompt.shape[1], dtype=jnp.int32)
    pos_s = past + jnp.arange(x_sample.shape[1], dtype=jnp.int32)
    xp, xs = x_prompt, x_sample
    st_p, st_s = [], []
    for l in range(DEPTH):
        def run(x, c, pos, attend):
            sh1, sc1, ga1, sh2, sc2, ga2 = modulation(c, w_ada[l], b_ada[l])
            h = rmsnorm(x, g_attn_norm[l]) * (1 + sc1) + sh1
            pr = project_mixers(h, pos, w_in[l], g_q_norm[l], w_q_up[l], g_kv_norm[l], w_uk[l], b_forget[l])
            ym, yf = attend(pr)
            x = x + ga1 * mix_output(ym, yf, g_mla_out[l], g_fox_out[l], w_out[l])
            h = rmsnorm(x, g_ffn_norm[l]) * (1 + sc2) + sh2
            x = x + ga2 * swiglu(h, w_gate[l], w_up[l], w_down[l])
            return x, (pr[2], pr[3], pr[5], pr[6], pr[7])

        xp, sp = run(xp, c_prompt, pos_p, lambda pr: prompt_mixers(pr, w_uv[l]))
        xs, ss = run(xs, c_sample, pos_s, lambda pr: sample_mixers(
            pr, w_uv[l], cache_mla_latent[l], cache_mla_krope[l], cache_fox_k[l], cache_fox_v[l],
            cache_fox_logf[l], page_table))
        st_p.append(sp)
        st_s.append(ss)
    y_prompt = rmsnorm(xp, g_final)
    y_sample = rmsnorm(xs, g_final)
    lat_p = jnp.stack([s[0] for s in st_p])
    krope_p = jnp.stack([s[1] for s in st_p])
    fk_p = jnp.stack([s[2] for s in st_p])
    fv_p = jnp.stack([s[3] for s in st_p])
    logf_p = jnp.stack([s[4] for s in st_p])
    lat_s = jnp.stack([s[0] for s in st_s])
    krope_s = jnp.stack([s[1] for s in st_s])
    fk_s = jnp.stack([s[2] for s in st_s])
    fv_s = jnp.stack([s[3] for s in st_s])
    logf_s = jnp.stack([s[4] for s in st_s])
    return (y_prompt, y_sample, lat_p, krope_p, fk_p, fv_p, logf_p, lat_s, krope_s, fk_s, fv_s, logf_s)
```

```python
import functools

import jax
import jax.numpy as jnp
from jax import lax
from jax.experimental import pallas as pl
from jax.experimental.pallas import tpu as pltpu

F32 = jnp.float32
BF16 = jnp.bfloat16

D_MODEL = 4096
HEAD_DIM = 128
MLA_HEADS = 16
FOX_HEADS = 16
FOX_KV_HEADS = 2
FOX_GROUP = FOX_HEADS // FOX_KV_HEADS
Q_LORA = 1536
KV_LORA = 512
ROPE_DIM = 64
ROPE_THETA = 10000.0
MLA_SCALE = (HEAD_DIM + ROPE_DIM) ** -0.5
FOX_SCALE = HEAD_DIM ** -0.5
N_MOD = 6
PAGE_SIZE = 128
EPS = 1e-6

LANES = 128
VMEM_LIMIT = 56 * 1024 * 1024

Z_CQ = 0
Z_CKV = Z_CQ + Q_LORA
Z_FQ = Z_CKV + KV_LORA
Z_FK = Z_FQ + FOX_HEADS * HEAD_DIM
Z_FV = Z_FK + FOX_KV_HEADS * HEAD_DIM
Z_KR = Z_FV + FOX_KV_HEADS * HEAD_DIM
Z_FF = Z_KR + LANES
Z_WIDTH = 5120
KVA_WIDTH = KV_LORA + LANES
QH = 2 * LANES
DECODE_PAGES = 16


def _params(*sem):
    return pltpu.CompilerParams(dimension_semantics=sem, vmem_limit_bytes=VMEM_LIMIT)


def _rms(x):
    return x * lax.rsqrt(jnp.mean(x * x, axis=-1, keepdims=True) + EPS)


def _rope_lanes(g, cs1, cs2):
    return g * cs1 + pltpu.roll(g, ROPE_DIM, axis=1) * cs2


def _ada_kernel(c_ref, w_ref, b_ref, o_ref):
    c = c_ref[...]
    a = (c * jax.nn.sigmoid(c)).astype(BF16)
    o_ref[...] = jnp.dot(a, w_ref[...].astype(BF16), preferred_element_type=F32) + b_ref[...]


def _ada(c, w, b):
    m, k = c.shape
    n = w.shape[1]
    tn = 512
    return pl.pallas_call(
        _ada_kernel,
        grid=(n // tn,),
        in_specs=[pl.BlockSpec((m, k), lambda j: (0, 0)),
                  pl.BlockSpec((k, tn), lambda j: (0, j)),
                  pl.BlockSpec((1, tn), lambda j: (0, j))],
        out_specs=pl.BlockSpec((m, tn), lambda j: (0, j)),
        out_shape=jax.ShapeDtypeStruct((m, n), F32),
        compiler_params=_params("arbitrary"),
        name="ada",
    )(c, w, b.reshape(1, n))


class _Mod:
    def __init__(self, arr, rows_per_batch):
        self.rows_per_batch = rows_per_batch
        if rows_per_batch == 1:
            self.arr = arr
        else:
            self.arr = arr.reshape(arr.shape[0], 1, arr.shape[1])

    def spec(self, tm, tn, row_only=False):
        if self.rows_per_batch == 1:
            if row_only:
                return pl.BlockSpec((tm, tn), lambda i: (i, 0))
            return pl.BlockSpec((tm, tn), lambda i, j: (i, j))
        per = self.rows_per_batch // tm
        if row_only:
            return pl.BlockSpec((None, 1, tn), lambda i: (i // per, 0, 0))
        return pl.BlockSpec((None, 1, tn), lambda i, j: (i // per, 0, j))


def _norm_mod_kernel(x_ref, g_ref, sc_ref, sh_ref, o_ref):
    y = _rms(x_ref[...]) * g_ref[...]
    o_ref[...] = (y * (1.0 + sc_ref[...]) + sh_ref[...]).astype(o_ref.dtype)


def _norm_kernel(x_ref, g_ref, o_ref):
    o_ref[...] = (_rms(x_ref[...]) * g_ref[...]).astype(o_ref.dtype)


def _norm_mod(x, g, sc, sh, tm):
    t, d = x.shape
    return pl.pallas_call(
        _norm_mod_kernel,
        grid=(t // tm,),
        in_specs=[pl.BlockSpec((tm, d), lambda i: (i, 0)),
                  pl.BlockSpec((1, d), lambda i: (0, 0)),
                  sc.spec(tm, d, row_only=True),
                  sh.spec(tm, d, row_only=True)],
        out_specs=pl.BlockSpec((tm, d), lambda i: (i, 0)),
        out_shape=jax.ShapeDtypeStruct((t, d), BF16),
        compiler_params=_params("arbitrary"),
        name="norm_mod",
    )(x, g.reshape(1, d), sc.arr, sh.arr)


def _norm(x, g, tm):
    t, d = x.shape
    return pl.pallas_call(
        _norm_kernel,
        grid=(t // tm,),
        in_specs=[pl.BlockSpec((tm, d), lambda i: (i, 0)),
                  pl.BlockSpec((1, d), lambda i: (0, 0))],
        out_specs=pl.BlockSpec((tm, d), lambda i: (i, 0)),
        out_shape=jax.ShapeDtypeStruct((t, d), F32),
        compiler_params=_params("arbitrary"),
        name="final_norm",
    )(x, g.reshape(1, d))


def _norm2_kernel(ym_ref, yf_ref, gm_ref, gf_ref, o_ref):
    w = ym_ref.shape[1]
    o_ref[:, :w] = (_rms(ym_ref[...]) * gm_ref[...]).astype(o_ref.dtype)
    o_ref[:, w:] = (_rms(yf_ref[...]) * gf_ref[...]).astype(o_ref.dtype)


def _norm2(ym, yf, gm, gf, tm):
    t, w = ym.shape
    return pl.pallas_call(
        _norm2_kernel,
        grid=(t // tm,),
        in_specs=[pl.BlockSpec((tm, w), lambda i: (i, 0)),
                  pl.BlockSpec((tm, w), lambda i: (i, 0)),
                  pl.BlockSpec((1, w), lambda i: (0, 0)),
                  pl.BlockSpec((1, w), lambda i: (0, 0))],
        out_specs=pl.BlockSpec((tm, 2 * w), lambda i: (i, 0)),
        out_shape=jax.ShapeDtypeStruct((t, 2 * w), BF16),
        compiler_params=_params("arbitrary"),
        name="mix_norm",
    )(ym, yf, gm.reshape(1, w), gf.reshape(1, w))


def _mm_kernel(a_ref, w_ref, *rest, epi, n_extra):
    o_ref = rest[n_extra]
    acc = jnp.dot(a_ref[...].astype(BF16), w_ref[...].astype(BF16),
                  preferred_element_type=F32)
    if epi is not None:
        acc = epi(acc, *rest[:n_extra])
    o_ref[...] = acc.astype(o_ref.dtype)


def _dense(a, w, *, tm, tn, out_dtype, name, epi=None, extras=()):
    m, k = a.shape
    n = w.shape[1]
    assert m % tm == 0 and n % tn == 0
    return pl.pallas_call(
        functools.partial(_mm_kernel, epi=epi, n_extra=len(extras)),
        grid=(m // tm, n // tn),
        in_specs=[pl.BlockSpec((tm, k), lambda i, j: (i, 0)),
                  pl.BlockSpec((k, tn), lambda i, j: (0, j))] + [s for _, s in extras],
        out_specs=pl.BlockSpec((tm, tn), lambda i, j: (i, j)),
        out_shape=jax.ShapeDtypeStruct((m, n), out_dtype),
        compiler_params=_params("arbitrary", "arbitrary"),
        name=name,
    )(a, w, *[x for x, _ in extras])


def _epi_residual(acc, x_ref, ga_ref):
    return x_ref[...] + ga_ref[...] * acc


def _epi_q_rope(acc, cs1_ref, cs2_ref):
    cs1 = cs1_ref[...]
    cs2 = cs2_ref[...]
    parts = []
    for h in range(acc.shape[1] // QH):
        parts.append(acc[:, h * QH:h * QH + LANES])
        parts.append(_rope_lanes(acc[:, h * QH + LANES:(h + 1) * QH], cs1, cs2))
    return jnp.concatenate(parts, axis=1) * MLA_SCALE


def _gate_up_kernel(a_ref, wg_ref, wu_ref, o_ref):
    a = a_ref[...]
    g = jnp.dot(a, wg_ref[...].astype(BF16), preferred_element_type=F32)
    u = jnp.dot(a, wu_ref[...].astype(BF16), preferred_element_type=F32)
    o_ref[...] = (g * jax.nn.sigmoid(g) * u).astype(o_ref.dtype)


def _gate_up(a, wg, wu, tm, tn):
    m, k = a.shape
    n = wg.shape[1]
    assert m % tm == 0 and n % tn == 0
    return pl.pallas_call(
        _gate_up_kernel,
        grid=(m // tm, n // tn),
        in_specs=[pl.BlockSpec((tm, k), lambda i, j: (i, 0)),
                  pl.BlockSpec((k, tn), lambda i, j: (0, j)),
                  pl.BlockSpec((k, tn), lambda i, j: (0, j))],
        out_specs=pl.BlockSpec((tm, tn), lambda i, j: (i, j)),
        out_shape=jax.ShapeDtypeStruct((m, n), BF16),
        compiler_params=_params("arbitrary", "arbitrary"),
        name="ffn_gate_up",
    )(a, wg, wu)


def _post_in_kernel(z_ref, gq_ref, gkv_ref, bf_ref, cs1_ref, cs2_ref,
                    cqn_ref, lat_ref, kva_ref, kr_ref, fq_ref, fk_ref, fv_ref,
                    fkb_ref, fvb_ref, lf_ref, lf128_ref):
    cqn_ref[...] = (_rms(z_ref[:, Z_CQ:Z_CKV]) * gq_ref[...]).astype(BF16)
    lat = _rms(z_ref[:, Z_CKV:Z_FQ]) * gkv_ref[...]
    lat_ref[...] = lat
    kr = _rope_lanes(z_ref[:, Z_KR:Z_FF], cs1_ref[...], cs2_ref[...])
    kr_ref[...] = kr[:, :ROPE_DIM]
    kva_ref[:, :KV_LORA] = lat.astype(BF16)
    kva_ref[:, KV_LORA:] = kr.astype(BF16)
    fq_ref[...] = (z_ref[:, Z_FQ:Z_FK] * FOX_SCALE).astype(BF16)
    fk = z_ref[:, Z_FK:Z_FV]
    fv = z_ref[:, Z_FV:Z_KR]
    fk_ref[...] = fk
    fv_ref[...] = fv
    fkb_ref[...] = fk.astype(BF16)
    fvb_ref[...] = fv.astype(BF16)
    x = z_ref[:, Z_FF:Z_FF + LANES] + bf_ref[...]
    lf = jnp.minimum(x, 0.0) - jnp.log1p(jnp.exp(-jnp.abs(x)))
    lane = lax.broadcasted_iota(jnp.int32, lf.shape, 1)
    lf = jnp.where(lane < FOX_HEADS, lf, 0.0)
    lf128_ref[...] = lf
    lf_ref[...] = lf[:, :FOX_HEADS]


def _post_in(z, g_q, g_kv, b_f, cs1, cs2, tm, pos_blocks):
    t = z.shape[0]
    kvw = FOX_KV_HEADS * HEAD_DIM
    row = lambda w: pl.BlockSpec((tm, w), lambda i: (i, 0))
    const = lambda w: pl.BlockSpec((1, w), lambda i: (0, 0))
    pos = pl.BlockSpec((tm, LANES), lambda i: (i % pos_blocks, 0))
    bf = jnp.zeros((1, LANES), F32).at[0, :FOX_HEADS].set(b_f)
    shapes = [((t, Q_LORA), BF16), ((t, KV_LORA), F32), ((t, KVA_WIDTH), BF16),
              ((t, ROPE_DIM), F32), ((t, FOX_HEADS * HEAD_DIM), BF16),
              ((t, kvw), F32), ((t, kvw), F32), ((t, kvw), BF16), ((t, kvw), BF16),
              ((t, FOX_HEADS), F32), ((t, LANES), F32)]
    return pl.pallas_call(
        _post_in_kernel,
        grid=(t // tm,),
        in_specs=[row(Z_WIDTH), const(Q_LORA), const(KV_LORA), const(LANES), pos, pos],
        out_specs=[row(s[1]) for s, _ in shapes],
        out_shape=[jax.ShapeDtypeStruct(s, dt) for s, dt in shapes],
        compiler_params=_params("arbitrary"),
        name="post_in",
    )(z, g_q.reshape(1, Q_LORA), g_kv.reshape(1, KV_LORA), bf, cs1, cs2)


def _split3(x):
    a1 = x.astype(BF16)
    r1 = x - a1.astype(F32)
    a2 = r1.astype(BF16)
    a3 = (r1 - a2.astype(F32)).astype(BF16)
    return a1, a2, a3


def _suffix_kernel(lf_ref, e_ref, et_ref, *, rows):
    s = lf_ref.shape[0]
    parts = _split3(lf_ref[...])
    blocks = []
    for r0 in range(0, s, rows):
        row = lax.broadcasted_iota(jnp.int32, (rows, s), 0) + r0
        col = lax.broadcasted_iota(jnp.int32, (rows, s), 1)
        upper = (col > row).astype(BF16)
        blocks.append(sum(jnp.dot(upper, a, preferred_element_type=F32) for a in parts))
    e = jnp.concatenate(blocks, axis=0)
    e_ref[...] = e[:, :FOX_HEADS]
    et_ref[...] = e.T[:FOX_HEADS, :]


def _suffix(lf128):
    b, s, _ = lf128.shape
    return pl.pallas_call(
        functools.partial(_suffix_kernel, rows=512),
        grid=(b,),
        in_specs=[pl.BlockSpec((None, s, LANES), lambda i: (i, 0, 0))],
        out_specs=[pl.BlockSpec((None, s, FOX_HEADS), lambda i: (i, 0, 0)),
                   pl.BlockSpec((None, FOX_HEADS, s), lambda i: (i, 0, 0))],
        out_shape=[jax.ShapeDtypeStruct((b, s, FOX_HEADS), F32),
                   jax.ShapeDtypeStruct((b, FOX_HEADS, s), F32)],
        compiler_params=_params("arbitrary"),
        name="suffix_logf",
    )(lf128)


def _flash(q_ref, k_ref, v_ref, o_ref, ek_ref, eq, *, blk):
    s_len = q_ref.shape[0]
    dv = v_ref.shape[1]
    row = lax.broadcasted_iota(jnp.int32, (blk, blk), 0)
    col = lax.broadcasted_iota(jnp.int32, (blk, blk), 1)
    causal = row >= col
    for qi in range(s_len // blk):
        q = q_ref[qi * blk:(qi + 1) * blk, :]
        eq_blk = None if eq is None else eq[qi * blk:(qi + 1) * blk, :]

        def step(ki, carry, diagonal):
            m, l, acc = carry
            k0 = ki * blk if diagonal else pl.multiple_of(ki * blk, blk)
            k = k_ref[pl.ds(k0, blk), :]
            v = v_ref[pl.ds(k0, blk), :]
            s = lax.dot_general(q, k, (((1,), (1,)), ((), ())), preferred_element_type=F32)
            if ek_ref is not None:
                s = s + ek_ref[pl.ds(ki, 1), :] - eq_blk
            if diagonal:
                s = jnp.where(causal, s, -jnp.inf)
            m_new = jnp.maximum(m, jnp.max(s, axis=-1, keepdims=True))
            alpha = jnp.exp(m - m_new)
            p = jnp.exp(s - m_new)
            l = alpha * l + jnp.sum(p, axis=-1, keepdims=True)
            acc = alpha * acc + jnp.dot(p.astype(BF16), v, preferred_element_type=F32)
            return m_new, l, acc

        carry = (jnp.full((blk, 1), -jnp.inf, F32), jnp.zeros((blk, 1), F32),
                 jnp.zeros((blk, dv), F32))
        if qi > 0:
            carry = lax.fori_loop(0, qi, lambda ki, c: step(ki, c, False), carry)
        _, l, acc = step(qi, carry, True)
        o_ref[qi * blk:(qi + 1) * blk, :] = acc / l


def _flash_mla_kernel(q_ref, k_ref, v_ref, o_ref, *, blk):
    _flash(q_ref, k_ref, v_ref, o_ref, None, None, blk=blk)


def _flash_fox_kernel(q_ref, k_ref, v_ref, ek_ref, e_ref, o_ref, *, blk):
    h = pl.program_id(1)
    e = e_ref[...]
    lane = lax.broadcasted_iota(jnp.int32, e.shape, 1)
    eq = jnp.sum(jnp.where(lane == h, e, 0.0), axis=-1, keepdims=True)
    _flash(q_ref, k_ref, v_ref, o_ref, ek_ref, eq, blk=blk)


def _flash_mla(q, kv, blk):
    b, s, _ = q.shape
    v_off = MLA_HEADS * QH // HEAD_DIM
    return pl.pallas_call(
        functools.partial(_flash_mla_kernel, blk=blk),
        grid=(b, MLA_HEADS),
        in_specs=[pl.BlockSpec((None, s, QH), lambda i, h: (i, 0, h)),
                  pl.BlockSpec((None, s, QH), lambda i, h: (i, 0, h)),
                  pl.BlockSpec((None, s, HEAD_DIM), lambda i, h: (i, 0, v_off + h))],
        out_specs=pl.BlockSpec((None, s, HEAD_DIM), lambda i, h: (i, 0, h)),
        out_shape=jax.ShapeDtypeStruct((b, s, MLA_HEADS * HEAD_DIM), F32),
        compiler_params=_params("arbitrary", "arbitrary"),
        name="flash_mla",
    )(q, kv, kv)


def _flash_fox(q, k, v, e, et, blk):
    b, s, _ = q.shape
    ek = et.reshape(b, FOX_HEADS, s // blk, blk)
    return pl.pallas_call(
        functools.partial(_flash_fox_kernel, blk=blk),
        grid=(b, FOX_HEADS),
        in_specs=[pl.BlockSpec((None, s, HEAD_DIM), lambda i, h: (i, 0, h)),
                  pl.BlockSpec((None, s, HEAD_DIM), lambda i, h: (i, 0, h // FOX_GROUP)),
                  pl.BlockSpec((None, s, HEAD_DIM), lambda i, h: (i, 0, h // FOX_GROUP)),
                  pl.BlockSpec((None, None, s // blk, blk), lambda i, h: (i, h, 0, 0)),
                  pl.BlockSpec((None, s, FOX_HEADS), lambda i, h: (i, 0, 0))],
        out_specs=pl.BlockSpec((None, s, HEAD_DIM), lambda i, h: (i, 0, h)),
        out_shape=jax.ShapeDtypeStruct((b, s, FOX_HEADS * HEAD_DIM), F32),
        compiler_params=_params("arbitrary", "arbitrary"),
        name="flash_fox",
    )(q, k, v, ek, e)


def _absorb_kernel(q_ref, w_ref, o_ref):
    o_ref[...] = lax.dot_general(q_ref[...], w_ref[...].astype(BF16),
                                 (((1,), (1,)), ((), ())),
                                 preferred_element_type=F32).astype(o_ref.dtype)


def _absorb(q256, w_uk2d):
    t = q256.shape[0]
    return pl.pallas_call(
        _absorb_kernel,
        grid=(MLA_HEADS,),
        in_specs=[pl.BlockSpec((t, HEAD_DIM), lambda h: (0, 2 * h)),
                  pl.BlockSpec((KV_LORA, HEAD_DIM), lambda h: (0, h))],
        out_specs=pl.BlockSpec((t, KV_LORA), lambda h: (0, h)),
        out_shape=jax.ShapeDtypeStruct((t, MLA_HEADS * KV_LORA), BF16),
        compiler_params=_params("arbitrary"),
        name="q_absorb",
    )(q256, w_uk2d)


def _value_up_kernel(o_ref_in, w_ref, o_ref):
    o_ref[...] = jnp.dot(o_ref_in[...].astype(BF16), w_ref[...].astype(BF16),
                         preferred_element_type=F32)


def _value_up(o_lat2d, w_uv2d):
    t = o_lat2d.shape[0]
    return pl.pallas_call(
        _value_up_kernel,
        grid=(MLA_HEADS,),
        in_specs=[pl.BlockSpec((t, KV_LORA), lambda h: (0, h)),
                  pl.BlockSpec((KV_LORA, HEAD_DIM), lambda h: (0, h))],
        out_specs=pl.BlockSpec((t, HEAD_DIM), lambda h: (0, h)),
        out_shape=jax.ShapeDtypeStruct((t, MLA_HEADS * HEAD_DIM), F32),
        compiler_params=_params("arbitrary"),
        name="value_up",
    )(o_lat2d, w_uv2d)


def _decode_kernel(pt_ref, qlat_ref, qrope_ref, fq_ref, ckvn_ref, krn_ref, fkn_ref, fvn_ref,
                   lfn_ref, lat_hbm, kr_hbm, fk_hbm, fv_hbm, lf_hbm,
                   olat_ref, ofox_ref,
                   lat_buf, kr_buf, fk_buf, fv_buf, lf_buf, sems,
                   m_m, l_m, acc_m, m_f, l_f, acc_f, carry, *, cp):
    b = pl.program_id(0)
    j = pl.program_id(1)
    nb = pl.num_programs(0)
    nch = pl.num_programs(1)
    t = b * nch + j
    slot = t % 2
    kc = cp * PAGE_SIZE
    nt = (((1,), (1,)), ((), ()))

    def chunk_copies(bb, jj, sl):
        first_page = (nch - 1 - jj) * cp
        out = []
        for p in range(cp):
            page = pt_ref[bb, first_page + p]
            for a, (hbm, buf) in enumerate(((lat_hbm, lat_buf), (kr_hbm, kr_buf), (fk_hbm, fk_buf),
                                            (fv_hbm, fv_buf), (lf_hbm, lf_buf))):
                out.append(pltpu.make_async_copy(hbm.at[page], buf.at[sl, p], sems.at[a, sl]))
        return out

    @pl.when(t == 0)
    def _():
        for c in chunk_copies(b, j, slot):
            c.start()

    @pl.when(t + 1 < nb * nch)
    def _():
        wrap = j + 1 == nch
        for c in chunk_copies(jnp.where(wrap, b + 1, b), jnp.where(wrap, 0, j + 1), 1 - slot):
            c.start()

    qlat = qlat_ref[...]
    qrope = qrope_ref[:, :ROPE_DIM]
    fq = fq_ref[...]

    @pl.when(j == 0)
    def _():
        ckvn = ckvn_ref[...].astype(BF16).astype(F32)
        krn = krn_ref[:, :ROPE_DIM].astype(BF16).astype(F32)
        m_m[...] = (jnp.sum(qlat.astype(F32) * ckvn, axis=-1, keepdims=True)
                    + jnp.sum(qrope.astype(F32) * krn, axis=-1, keepdims=True))
        l_m[...] = jnp.ones_like(l_m)
        acc_m[...] = jnp.broadcast_to(ckvn, acc_m.shape)
        fkn = fkn_ref[...].astype(BF16).astype(F32)
        fvn = fvn_ref[...].astype(BF16).astype(F32)
        m_f[...] = jnp.sum(fq.astype(F32) * fkn, axis=-1, keepdims=True)
        l_f[...] = jnp.ones_like(l_f)
        acc_f[...] = jnp.broadcast_to(fvn, acc_f.shape)
        carry[...] = lfn_ref[...]

    for c in chunk_copies(b, j, slot):
        c.wait()

    latb = lat_buf[slot].reshape(kc, KV_LORA).astype(BF16)
    krb = kr_buf[slot].reshape(kc, ROPE_DIM).astype(BF16)
    s = (lax.dot_general(qlat, latb, nt, preferred_element_type=F32)
         + lax.dot_general(qrope, krb, nt, preferred_element_type=F32))
    m_old = m_m[...]
    m_new = jnp.maximum(m_old, jnp.max(s, axis=-1, keepdims=True))
    alpha = jnp.exp(m_old - m_new)
    p = jnp.exp(s - m_new)
    l_m[...] = alpha * l_m[...] + jnp.sum(p, axis=-1, keepdims=True)
    acc_m[...] = alpha * acc_m[...] + jnp.dot(p.astype(BF16), latb, preferred_element_type=F32)
    m_m[...] = m_new

    lf = lf_buf[slot]
    x1, x2, x3 = _split3(lf.reshape(cp * FOX_HEADS, PAGE_SIZE))
    jr = lax.broadcasted_iota(jnp.int32, (PAGE_SIZE, PAGE_SIZE), 0)
    kcol = lax.broadcasted_iota(jnp.int32, (PAGE_SIZE, PAGE_SIZE), 1)
    later = (jr > kcol).astype(BF16)
    within = (jnp.dot(x1, later, preferred_element_type=F32)
              + jnp.dot(x2, later, preferred_element_type=F32)
              + jnp.dot(x3, later, preferred_element_type=F32))
    off = carry[...]
    tiles = [None] * cp
    for pg in range(cp - 1, -1, -1):
        tiles[pg] = within[pg * FOX_HEADS:(pg + 1) * FOX_HEADS, :] + off
        off = off + jnp.sum(lf[pg], axis=-1, keepdims=True)
    carry[...] = off
    bias = jnp.concatenate(tiles, axis=1)

    fkb = fk_buf[slot].reshape(kc, FOX_KV_HEADS * HEAD_DIM).astype(BF16)
    fvb = fv_buf[slot].reshape(kc, FOX_KV_HEADS * HEAD_DIM).astype(BF16)
    s = lax.dot_general(fq, fkb, nt, preferred_element_type=F32) + bias
    m_old = m_f[...]
    m_new = jnp.maximum(m_old, jnp.max(s, axis=-1, keepdims=True))
    alpha = jnp.exp(m_old - m_new)
    p = jnp.exp(s - m_new)
    l_f[...] = alpha * l_f[...] + jnp.sum(p, axis=-1, keepdims=True)
    acc_f[...] = alpha * acc_f[...] + jnp.dot(p.astype(BF16), fvb, preferred_element_type=F32)
    m_f[...] = m_new

    @pl.when(j == nch - 1)
    def _():
        olat_ref[...] = acc_m[...] / l_m[...]
        af = acc_f[...]
        head = lax.broadcasted_iota(jnp.int32, (FOX_HEADS, HEAD_DIM), 0)
        own = jnp.where(head < FOX_GROUP, af[:, :HEAD_DIM], af[:, HEAD_DIM:])
        ofox_ref[...] = own / l_f[...]


def _decode_attention(page_table, qlat, qrope, fq_bd, ckv_new, kr_new, fk_new, fv_new, lf_new,
                      lat_c, kr_c, fk_c, fv_c, lf_c):
    nb, n_pages = page_table.shape
    cp = DECODE_PAGES
    kvw = FOX_KV_HEADS * HEAD_DIM
    per_b = lambda *shape: pl.BlockSpec((None,) + shape, lambda b, j, pt: (b, 0, 0))
    hbm = pl.BlockSpec(memory_space=pl.ANY)
    grid_spec = pltpu.PrefetchScalarGridSpec(
        num_scalar_prefetch=1,
        grid=(nb, n_pages // cp),
        in_specs=[per_b(MLA_HEADS, KV_LORA), per_b(MLA_HEADS, LANES), per_b(FOX_HEADS, kvw),
                  per_b(1, KV_LORA), per_b(1, LANES), per_b(1, kvw), per_b(1, kvw),
                  per_b(FOX_HEADS, 1), hbm, hbm, hbm, hbm, hbm],
        out_specs=[per_b(MLA_HEADS, KV_LORA), per_b(FOX_HEADS, HEAD_DIM)],
        scratch_shapes=[
            pltpu.VMEM((2, cp, PAGE_SIZE, KV_LORA), F32),
            pltpu.VMEM((2, cp, PAGE_SIZE, ROPE_DIM), F32),
            pltpu.VMEM((2, cp, PAGE_SIZE, kvw), F32),
            pltpu.VMEM((2, cp, PAGE_SIZE, kvw), F32),
            pltpu.VMEM((2, cp, FOX_HEADS, PAGE_SIZE), F32),
            pltpu.SemaphoreType.DMA((5, 2)),
            pltpu.VMEM((MLA_HEADS, 1), F32), pltpu.VMEM((MLA_HEADS, 1), F32),
            pltpu.VMEM((MLA_HEADS, KV_LORA), F32),
            pltpu.VMEM((FOX_HEADS, 1), F32), pltpu.VMEM((FOX_HEADS, 1), F32),
            pltpu.VMEM((FOX_HEADS, kvw), F32),
            pltpu.VMEM((FOX_HEADS, 1), F32),
        ],
    )
    return pl.pallas_call(
        functools.partial(_decode_kernel, cp=cp),
        grid_spec=grid_spec,
        out_shape=[jax.ShapeDtypeStruct((nb, MLA_HEADS, KV_LORA), F32),
                   jax.ShapeDtypeStruct((nb, FOX_HEADS, HEAD_DIM), F32)],
        compiler_params=_params("arbitrary", "arbitrary"),
        name="decode_attention",
    )(page_table, qlat, qrope, fq_bd, ckv_new, kr_new, fk_new, fv_new, lf_new,
      lat_c, kr_c, fk_c, fv_c, lf_c)


def _swap_halves(x):
    half = x.shape[-1] // 2
    return jnp.concatenate([x[..., half:], x[..., :half]], axis=-1)


def _relayout_w_in(w):
    o = 0
    segs = {}
    for name, width in (("cq", Q_LORA), ("ckv", KV_LORA), ("kr", ROPE_DIM),
                        ("fq", FOX_HEADS * HEAD_DIM), ("fk", FOX_KV_HEADS * HEAD_DIM),
                        ("fv", FOX_KV_HEADS * HEAD_DIM), ("ff", FOX_HEADS)):
        segs[name] = w[:, o:o + width]
        o += width
    zeros = lambda n: jnp.zeros((w.shape[0], n), w.dtype)
    cols = [segs["cq"], segs["ckv"], segs["fq"], segs["fk"], segs["fv"],
            segs["kr"], _swap_halves(segs["kr"]), segs["ff"], zeros(LANES - FOX_HEADS),
            zeros(Z_WIDTH - Z_FF - LANES)]
    return jnp.concatenate(cols, axis=1).astype(BF16)


def _relayout_w_q_up(w):
    nope, rope = w[..., :HEAD_DIM], w[..., HEAD_DIM:]
    ext = jnp.concatenate([nope, rope, _swap_halves(rope)], axis=-1)
    return ext.reshape(w.shape[0], MLA_HEADS * QH).astype(BF16)


def _build_w_kv(w_uk, w_uv):
    wk = jnp.zeros((KVA_WIDTH, MLA_HEADS, QH), F32)
    wk = wk.at[:KV_LORA, :, :HEAD_DIM].set(w_uk)
    eye = jnp.broadcast_to(jnp.eye(ROPE_DIM, dtype=F32)[:, None, :], (ROPE_DIM, MLA_HEADS, ROPE_DIM))
    wk = wk.at[KV_LORA:KV_LORA + ROPE_DIM, :, HEAD_DIM:HEAD_DIM + ROPE_DIM].set(eye)
    wv = jnp.zeros((KVA_WIDTH, MLA_HEADS, HEAD_DIM), F32).at[:KV_LORA].set(w_uv)
    return jnp.concatenate([wk.reshape(KVA_WIDTH, -1), wv.reshape(KVA_WIDTH, -1)], axis=1).astype(BF16)


def _rope_tables(pos):
    half = ROPE_DIM // 2
    inv_freq = jnp.power(ROPE_THETA, -jnp.arange(half, dtype=F32) * 2.0 / ROPE_DIM)
    ang = pos.astype(F32)[:, None] * inv_freq[None, :]
    cos, sin = jnp.cos(ang), jnp.sin(ang)
    z = jnp.zeros((pos.shape[0], LANES - ROPE_DIM), F32)
    return (jnp.concatenate([cos, cos, z], axis=1), jnp.concatenate([-sin, sin, z], axis=1))


def _layer(x, mods, rows_per_batch, pos_tables, pos_blocks, tm, w, attend):
    sh1, sc1, ga1, sh2, sc2, ga2 = [_Mod(m, rows_per_batch) for m in mods]
    cs1, cs2 = pos_tables
    tm_small = min(tm, 256)
    h = _norm_mod(x, w["g_attn"], sc1, sh1, tm_small)
    z = _dense(h, w["w_in"], tm=tm, tn=512, out_dtype=F32, name="in_proj")
    (cqn, lat, kva, kr, fq, fk, fv, fkb, fvb, lf, lf128) = _post_in(
        z, w["g_q"], w["g_kv"], w["b_f"], cs1, cs2, tm_small, pos_blocks * (tm // tm_small) if pos_blocks > 1 else 1)
    pos_spec = pl.BlockSpec((tm, LANES), lambda i, j: (i % pos_blocks, 0))
    q256 = _dense(cqn, w["w_q"], tm=tm, tn=512, out_dtype=BF16, name="q_up",
                  epi=_epi_q_rope, extras=((cs1, pos_spec), (cs2, pos_spec)))
    ym, yf = attend(q256, kva, fq, fkb, fvb, lf128, lat, kr, fk, fv, lf)
    y = _norm2(ym, yf, w["g_mla"], w["g_fox"], tm_small)
    x = _dense(y, w["w_out"], tm=tm, tn=512, out_dtype=F32, name="out_proj", epi=_epi_residual,
               extras=((x, pl.BlockSpec((tm, 512), lambda i, j: (i, j))), (ga1.arr, ga1.spec(tm, 512))))
    h = _norm_mod(x, w["g_ffn"], sc2, sh2, tm_small)
    a = _gate_up(h, w["w_gate"], w["w_up"], tm, 256)
    tmd = min(tm, 512)
    x = _dense(a, w["w_down"], tm=tmd, tn=256, out_dtype=F32, name="ffn_down", epi=_epi_residual,
               extras=((x, pl.BlockSpec((tmd, 256), lambda i, j: (i, j))), (ga2.arr, ga2.spec(tmd, 256))))
    return x, (lat, kr, fk, fv, lf)


def kernel(x_prompt, x_sample, c_prompt, c_sample, cache_mla_latent, cache_mla_krope, cache_fox_k,
           cache_fox_v, cache_fox_logf, page_table, w_ada, b_ada, g_attn_norm, w_in, g_q_norm, w_q_up,
           g_kv_norm, w_uk, w_uv, b_forget, g_mla_out, g_fox_out, w_out, g_ffn_norm, w_gate, w_up,
           w_down, g_final):
    nbp, seq, d = x_prompt.shape
    nbs = x_sample.shape[0]
    depth = w_ada.shape[0]
    n_phys = cache_mla_latent.shape[1]
    past = page_table.shape[1] * PAGE_SIZE
    kvw = FOX_KV_HEADS * HEAD_DIM

    xp = x_prompt.reshape(nbp * seq, d)
    xs = x_sample.reshape(nbs, d)
    c_all = jnp.concatenate([c_prompt, c_sample], axis=0)
    tables_p = _rope_tables(jnp.arange(seq, dtype=jnp.int32))
    tables_s = tuple(jnp.broadcast_to(t, (nbs, LANES))
                     for t in _rope_tables(past + jnp.arange(1, dtype=jnp.int32)))
    tm_p = 1024
    st_p, st_s = [], []
    for l in range(depth):
        w = dict(g_attn=g_attn_norm[l], w_in=_relayout_w_in(w_in[l]), g_q=g_q_norm[l],
                 w_q=_relayout_w_q_up(w_q_up[l]), g_kv=g_kv_norm[l], b_f=b_forget[l],
                 g_mla=g_mla_out[l], g_fox=g_fox_out[l], w_out=w_out[l], g_ffn=g_ffn_norm[l],
                 w_gate=w_gate[l], w_up=w_up[l], w_down=w_down[l].astype(BF16))
        w_kv = _build_w_kv(w_uk[l], w_uv[l])
        w_uk2d = w_uk[l].reshape(KV_LORA, MLA_HEADS * HEAD_DIM)
        w_uv2d = w_uv[l].reshape(KV_LORA, MLA_HEADS * HEAD_DIM)
        mod = _ada(c_all, w_ada[l], b_ada[l])
        mods_p = [mod[:nbp, i * d:(i + 1) * d] for i in range(N_MOD)]
        mods_s = [mod[nbp:, i * d:(i + 1) * d] for i in range(N_MOD)]

        def attend_prompt(q256, kva, fq, fkb, fvb, lf128, *_):
            kv = _dense(kva, w_kv, tm=tm_p, tn=512, out_dtype=BF16, name="kv_up")
            ym = _flash_mla(q256.reshape(nbp, seq, -1), kv.reshape(nbp, seq, -1), 512)
            e, et = _suffix(lf128.reshape(nbp, seq, LANES))
            yf = _flash_fox(fq.reshape(nbp, seq, -1), fkb.reshape(nbp, seq, kvw),
                            fvb.reshape(nbp, seq, kvw), e, et, 512)
            return ym.reshape(nbp * seq, -1), yf.reshape(nbp * seq, -1)

        def attend_sample(q256, kva, fq, fkb, fvb, lf128, lat, kr, fk, fv, lf):
            qlat = _absorb(q256, w_uk2d).reshape(nbs, MLA_HEADS, KV_LORA)
            qrope = q256.reshape(nbs, MLA_HEADS, QH)[:, :, LANES:]
            fq4 = fq.reshape(nbs, FOX_KV_HEADS, FOX_GROUP, HEAD_DIM)
            own = jnp.eye(FOX_KV_HEADS, dtype=fq.dtype)[None, :, None, :, None]
            fq_bd = (fq4[:, :, :, None, :] * own).reshape(nbs, FOX_HEADS, kvw)
            o_lat, o_fox = _decode_attention(
                page_table, qlat, qrope, fq_bd,
                lat.reshape(nbs, 1, KV_LORA), kva[:, KV_LORA:].astype(F32).reshape(nbs, 1, LANES),
                fk.reshape(nbs, 1, kvw), fv.reshape(nbs, 1, kvw), lf.reshape(nbs, FOX_HEADS, 1),
                cache_mla_latent[l], cache_mla_krope[l],
                cache_fox_k[l].reshape(n_phys, PAGE_SIZE, kvw),
                cache_fox_v[l].reshape(n_phys, PAGE_SIZE, kvw),
                jnp.swapaxes(cache_fox_logf[l], 1, 2))
            ym = _value_up(o_lat.reshape(nbs, MLA_HEADS * KV_LORA), w_uv2d)
            return ym, o_fox.reshape(nbs, FOX_HEADS * HEAD_DIM)

        xp, sp = _layer(xp, mods_p, seq, tables_p, seq // tm_p, tm_p, w, attend_prompt)
        xs, ss = _layer(xs, mods_s, 1, tables_s, 1, nbs, w, attend_sample)
        st_p.append(sp)
        st_s.append(ss)

    y_prompt = _norm(xp, g_final, 256).reshape(nbp, seq, d)
    y_sample = _norm(xs, g_final, nbs).reshape(nbs, 1, d)

    def stack(states, idx, shape):
        return jnp.stack([s[idx].reshape(shape) for s in states])

    outs = [y_prompt, y_sample]
    for states, lead in ((st_p, (nbp, seq)), (st_s, (nbs, 1))):
        outs += [stack(states, 0, lead + (KV_LORA,)), stack(states, 1, lead + (ROPE_DIM,)),
                 stack(states, 2, lead + (FOX_KV_HEADS, HEAD_DIM)),
                 stack(states, 3, lead + (FOX_KV_HEADS, HEAD_DIM)),
                 stack(states, 4, lead + (FOX_HEADS,))]
    return tuple(outs)
```

```python
import functools

import jax
import jax.numpy as jnp
from jax import lax
from jax.experimental import pallas as pl
from jax.experimental.pallas import tpu as pltpu

F32 = jnp.float32
BF16 = jnp.bfloat16

D_MODEL = 4096
HEAD_DIM = 128
MLA_HEADS = 16
FOX_HEADS = 16
FOX_KV_HEADS = 2
FOX_GROUP = FOX_HEADS // FOX_KV_HEADS
Q_LORA = 1536
KV_LORA = 512
ROPE_DIM = 64
ROPE_THETA = 10000.0
MLA_SCALE = (HEAD_DIM + ROPE_DIM) ** -0.5
FOX_SCALE = HEAD_DIM ** -0.5
N_MOD = 6
PAGE_SIZE = 128
EPS = 1e-6

LANES = 128
VMEM_LIMIT = 56 * 1024 * 1024

Z_CQ = 0
Z_CKV = Z_CQ + Q_LORA
Z_FQ = Z_CKV + KV_LORA
Z_FK = Z_FQ + FOX_HEADS * HEAD_DIM
Z_FV = Z_FK + FOX_KV_HEADS * HEAD_DIM
Z_KR = Z_FV + FOX_KV_HEADS * HEAD_DIM
Z_FF = Z_KR + LANES
Z_WIDTH = 5120
KVA_WIDTH = KV_LORA + LANES
QH = 2 * LANES
DECODE_PAGES = 16
DECODE_SLOTS = 3


def _params(*sem):
    return pltpu.CompilerParams(dimension_semantics=sem, vmem_limit_bytes=VMEM_LIMIT)


def _rms(x):
    return x * lax.rsqrt(jnp.mean(x * x, axis=-1, keepdims=True) + EPS)


def _rope_lanes(g, cs1, cs2):
    return g * cs1 + pltpu.roll(g, ROPE_DIM, axis=1) * cs2


def _ada_kernel(c_ref, w_ref, b_ref, o_ref):
    c = c_ref[...]
    a = (c * jax.nn.sigmoid(c)).astype(BF16)
    o_ref[...] = jnp.dot(a, w_ref[...].astype(BF16), preferred_element_type=F32) + b_ref[...]


def _ada(c, w, b):
    m, k = c.shape
    n = w.shape[1]
    tn = 512
    return pl.pallas_call(
        _ada_kernel,
        grid=(n // tn,),
        in_specs=[pl.BlockSpec((m, k), lambda j: (0, 0)),
                  pl.BlockSpec((k, tn), lambda j: (0, j)),
                  pl.BlockSpec((1, tn), lambda j: (0, j))],
        out_specs=pl.BlockSpec((m, tn), lambda j: (0, j)),
        out_shape=jax.ShapeDtypeStruct((m, n), F32),
        compiler_params=_params("arbitrary"),
        name="ada",
    )(c, w, b.reshape(1, n))


class _Mod:
    def __init__(self, arr, rows_per_batch):
        self.rows_per_batch = rows_per_batch
        if rows_per_batch == 1:
            self.arr = arr
        else:
            self.arr = arr.reshape(arr.shape[0], 1, arr.shape[1])

    def spec(self, tm, tn, row_only=False):
        if self.rows_per_batch == 1:
            if row_only:
                return pl.BlockSpec((tm, tn), lambda i: (i, 0))
            return pl.BlockSpec((tm, tn), lambda i, j: (i, j))
        per = self.rows_per_batch // tm
        if row_only:
            return pl.BlockSpec((None, 1, tn), lambda i: (i // per, 0, 0))
        return pl.BlockSpec((None, 1, tn), lambda i, j: (i // per, 0, j))


def _norm_mod_kernel(x_ref, g_ref, sc_ref, sh_ref, o_ref):
    y = _rms(x_ref[...]) * g_ref[...]
    o_ref[...] = (y * (1.0 + sc_ref[...]) + sh_ref[...]).astype(o_ref.dtype)


def _norm_kernel(x_ref, g_ref, o_ref):
    o_ref[...] = (_rms(x_ref[...]) * g_ref[...]).astype(o_ref.dtype)


def _norm_mod(x, g, sc, sh, tm):
    t, d = x.shape
    return pl.pallas_call(
        _norm_mod_kernel,
        grid=(t // tm,),
        in_specs=[pl.BlockSpec((tm, d), lambda i: (i, 0)),
                  pl.BlockSpec((1, d), lambda i: (0, 0)),
                  sc.spec(tm, d, row_only=True),
                  sh.spec(tm, d, row_only=True)],
        out_specs=pl.BlockSpec((tm, d), lambda i: (i, 0)),
        out_shape=jax.ShapeDtypeStruct((t, d), BF16),
        compiler_params=_params("arbitrary"),
        name="norm_mod",
    )(x, g.reshape(1, d), sc.arr, sh.arr)


def _norm(x, g, tm):
    t, d = x.shape
    return pl.pallas_call(
        _norm_kernel,
        grid=(t // tm,),
        in_specs=[pl.BlockSpec((tm, d), lambda i: (i, 0)),
                  pl.BlockSpec((1, d), lambda i: (0, 0))],
        out_specs=pl.BlockSpec((tm, d), lambda i: (i, 0)),
        out_shape=jax.ShapeDtypeStruct((t, d), F32),
        compiler_params=_params("arbitrary"),
        name="final_norm",
    )(x, g.reshape(1, d))


def _norm2_kernel(ym_ref, yf_ref, gm_ref, gf_ref, o_ref):
    w = ym_ref.shape[1]
    o_ref[:, :w] = (_rms(ym_ref[...]) * gm_ref[...]).astype(o_ref.dtype)
    o_ref[:, w:] = (_rms(yf_ref[...]) * gf_ref[...]).astype(o_ref.dtype)


def _norm2(ym, yf, gm, gf, tm):
    t, w = ym.shape
    return pl.pallas_call(
        _norm2_kernel,
        grid=(t // tm,),
        in_specs=[pl.BlockSpec((tm, w), lambda i: (i, 0)),
                  pl.BlockSpec((tm, w), lambda i: (i, 0)),
                  pl.BlockSpec((1, w), lambda i: (0, 0)),
                  pl.BlockSpec((1, w), lambda i: (0, 0))],
        out_specs=pl.BlockSpec((tm, 2 * w), lambda i: (i, 0)),
        out_shape=jax.ShapeDtypeStruct((t, 2 * w), BF16),
        compiler_params=_params("arbitrary"),
        name="mix_norm",
    )(ym, yf, gm.reshape(1, w), gf.reshape(1, w))


def _mm_kernel(a_ref, w_ref, *rest, epi, n_extra):
    o_ref = rest[n_extra]
    acc = jnp.dot(a_ref[...].astype(BF16), w_ref[...].astype(BF16),
                  preferred_element_type=F32)
    if epi is not None:
        acc = epi(acc, *rest[:n_extra])
    o_ref[...] = acc.astype(o_ref.dtype)


def _dense(a, w, *, tm, tn, out_dtype, name, epi=None, extras=()):
    m, k = a.shape
    n = w.shape[1]
    assert m % tm == 0 and n % tn == 0
    return pl.pallas_call(
        functools.partial(_mm_kernel, epi=epi, n_extra=len(extras)),
        grid=(m // tm, n // tn),
        in_specs=[pl.BlockSpec((tm, k), lambda i, j: (i, 0)),
                  pl.BlockSpec((k, tn), lambda i, j: (0, j))] + [s for _, s in extras],
        out_specs=pl.BlockSpec((tm, tn), lambda i, j: (i, j)),
        out_shape=jax.ShapeDtypeStruct((m, n), out_dtype),
        compiler_params=_params("arbitrary", "arbitrary"),
        name=name,
    )(a, w, *[x for x, _ in extras])


def _epi_residual(acc, x_ref, ga_ref):
    return x_ref[...] + ga_ref[...] * acc


def _epi_q_rope(acc, cs1_ref, cs2_ref):
    cs1 = cs1_ref[...]
    cs2 = cs2_ref[...]
    parts = []
    for h in range(acc.shape[1] // QH):
        parts.append(acc[:, h * QH:h * QH + LANES])
        parts.append(_rope_lanes(acc[:, h * QH + LANES:(h + 1) * QH], cs1, cs2))
    return jnp.concatenate(parts, axis=1) * MLA_SCALE


def _gate_up_kernel(a_ref, wg_ref, wu_ref, o_ref):
    a = a_ref[...]
    g = jnp.dot(a, wg_ref[...].astype(BF16), preferred_element_type=F32)
    u = jnp.dot(a, wu_ref[...].astype(BF16), preferred_element_type=F32)
    o_ref[...] = (g * jax.nn.sigmoid(g) * u).astype(o_ref.dtype)


def _gate_up(a, wg, wu, tm, tn):
    m, k = a.shape
    n = wg.shape[1]
    assert m % tm == 0 and n % tn == 0
    return pl.pallas_call(
        _gate_up_kernel,
        grid=(m // tm, n // tn),
        in_specs=[pl.BlockSpec((tm, k), lambda i, j: (i, 0)),
                  pl.BlockSpec((k, tn), lambda i, j: (0, j)),
                  pl.BlockSpec((k, tn), lambda i, j: (0, j))],
        out_specs=pl.BlockSpec((tm, tn), lambda i, j: (i, j)),
        out_shape=jax.ShapeDtypeStruct((m, n), BF16),
        compiler_params=_params("arbitrary", "arbitrary"),
        name="ffn_gate_up",
    )(a, wg, wu)


def _post_in_kernel(z_ref, gq_ref, gkv_ref, bf_ref, cs1_ref, cs2_ref,
                    cqn_ref, lat_ref, kva_ref, kr_ref, fq_ref, fk_ref, fv_ref,
                    fkb_ref, fvb_ref, lf_ref, lf128_ref):
    cqn_ref[...] = (_rms(z_ref[:, Z_CQ:Z_CKV]) * gq_ref[...]).astype(BF16)
    lat = _rms(z_ref[:, Z_CKV:Z_FQ]) * gkv_ref[...]
    lat_ref[...] = lat
    kr = _rope_lanes(z_ref[:, Z_KR:Z_FF], cs1_ref[...], cs2_ref[...])
    kr_ref[...] = kr[:, :ROPE_DIM]
    kva_ref[:, :KV_LORA] = lat.astype(BF16)
    kva_ref[:, KV_LORA:] = kr.astype(BF16)
    fq_ref[...] = (z_ref[:, Z_FQ:Z_FK] * FOX_SCALE).astype(BF16)
    fk = z_ref[:, Z_FK:Z_FV]
    fv = z_ref[:, Z_FV:Z_KR]
    fk_ref[...] = fk
    fv_ref[...] = fv
    fkb_ref[...] = fk.astype(BF16)
    fvb_ref[...] = fv.astype(BF16)
    x = z_ref[:, Z_FF:Z_FF + LANES] + bf_ref[...]
    lf = jnp.minimum(x, 0.0) - jnp.log1p(jnp.exp(-jnp.abs(x)))
    lane = lax.broadcasted_iota(jnp.int32, lf.shape, 1)
    lf = jnp.where(lane < FOX_HEADS, lf, 0.0)
    lf128_ref[...] = lf
    lf_ref[...] = lf[:, :FOX_HEADS]


def _post_in(z, g_q, g_kv, b_f, cs1, cs2, tm, pos_blocks):
    t = z.shape[0]
    kvw = FOX_KV_HEADS * HEAD_DIM
    row = lambda w: pl.BlockSpec((tm, w), lambda i: (i, 0))
    const = lambda w: pl.BlockSpec((1, w), lambda i: (0, 0))
    pos = pl.BlockSpec((tm, LANES), lambda i: (i % pos_blocks, 0))
    bf = jnp.zeros((1, LANES), F32).at[0, :FOX_HEADS].set(b_f)
    shapes = [((t, Q_LORA), BF16), ((t, KV_LORA), F32), ((t, KVA_WIDTH), BF16),
              ((t, ROPE_DIM), F32), ((t, FOX_HEADS * HEAD_DIM), BF16),
              ((t, kvw), F32), ((t, kvw), F32), ((t, kvw), BF16), ((t, kvw), BF16),
              ((t, FOX_HEADS), F32), ((t, LANES), F32)]
    return pl.pallas_call(
        _post_in_kernel,
        grid=(t // tm,),
        in_specs=[row(Z_WIDTH), const(Q_LORA), const(KV_LORA), const(LANES), pos, pos],
        out_specs=[row(s[1]) for s, _ in shapes],
        out_shape=[jax.ShapeDtypeStruct(s, dt) for s, dt in shapes],
        compiler_params=_params("arbitrary"),
        name="post_in",
    )(z, g_q.reshape(1, Q_LORA), g_kv.reshape(1, KV_LORA), bf, cs1, cs2)


def _split3(x):
    a1 = x.astype(BF16)
    r1 = x - a1.astype(F32)
    a2 = r1.astype(BF16)
    a3 = (r1 - a2.astype(F32)).astype(BF16)
    return a1, a2, a3


def _suffix_kernel(lf_ref, e_ref, et_ref, *, rows):
    s = lf_ref.shape[0]
    parts = _split3(lf_ref[...])
    blocks = []
    for r0 in range(0, s, rows):
        row = lax.broadcasted_iota(jnp.int32, (rows, s), 0) + r0
        col = lax.broadcasted_iota(jnp.int32, (rows, s), 1)
        upper = (col > row).astype(BF16)
        blocks.append(sum(jnp.dot(upper, a, preferred_element_type=F32) for a in parts))
    e = jnp.concatenate(blocks, axis=0)
    e_ref[...] = e[:, :FOX_HEADS]
    et_ref[...] = e.T[:FOX_HEADS, :]


def _suffix(lf128):
    b, s, _ = lf128.shape
    return pl.pallas_call(
        functools.partial(_suffix_kernel, rows=512),
        grid=(b,),
        in_specs=[pl.BlockSpec((None, s, LANES), lambda i: (i, 0, 0))],
        out_specs=[pl.BlockSpec((None, s, FOX_HEADS), lambda i: (i, 0, 0)),
                   pl.BlockSpec((None, FOX_HEADS, s), lambda i: (i, 0, 0))],
        out_shape=[jax.ShapeDtypeStruct((b, s, FOX_HEADS), F32),
                   jax.ShapeDtypeStruct((b, FOX_HEADS, s), F32)],
        compiler_params=_params("arbitrary"),
        name="suffix_logf",
    )(lf128)


def _flash(q_ref, k_ref, v_ref, o_ref, ek_ref, eq, *, blk):
    s_len = q_ref.shape[0]
    dv = v_ref.shape[1]
    row = lax.broadcasted_iota(jnp.int32, (blk, blk), 0)
    col = lax.broadcasted_iota(jnp.int32, (blk, blk), 1)
    causal = row >= col
    for qi in range(s_len // blk):
        q = q_ref[qi * blk:(qi + 1) * blk, :]
        eq_blk = None if eq is None else eq[qi * blk:(qi + 1) * blk, :]

        def step(ki, carry, diagonal):
            m, l, acc = carry
            k0 = ki * blk if diagonal else pl.multiple_of(ki * blk, blk)
            k = k_ref[pl.ds(k0, blk), :]
            v = v_ref[pl.ds(k0, blk), :]
            s = lax.dot_general(q, k, (((1,), (1,)), ((), ())), preferred_element_type=F32)
            if ek_ref is not None:
                s = s + ek_ref[pl.ds(ki, 1), :] - eq_blk
            if diagonal:
                s = jnp.where(causal, s, -jnp.inf)
            m_new = jnp.maximum(m, jnp.max(s, axis=-1, keepdims=True))
            alpha = jnp.exp(m - m_new)
            p = jnp.exp(s - m_new)
            l = alpha * l + jnp.sum(p, axis=-1, keepdims=True)
            acc = alpha * acc + jnp.dot(p.astype(BF16), v, preferred_element_type=F32)
            return m_new, l, acc

        carry = (jnp.full((blk, 1), -jnp.inf, F32), jnp.zeros((blk, 1), F32),
                 jnp.zeros((blk, dv), F32))
        if qi > 0:
            carry = lax.fori_loop(0, qi, lambda ki, c: step(ki, c, False), carry)
        _, l, acc = step(qi, carry, True)
        o_ref[qi * blk:(qi + 1) * blk, :] = acc / l


def _flash_mla_kernel(q_ref, k_ref, v_ref, o_ref, *, blk):
    _flash(q_ref, k_ref, v_ref, o_ref, None, None, blk=blk)


def _flash_fox_kernel(q_ref, k_ref, v_ref, ek_ref, e_ref, o_ref, *, blk):
    h = pl.program_id(1)
    e = e_ref[...]
    lane = lax.broadcasted_iota(jnp.int32, e.shape, 1)
    eq = jnp.sum(jnp.where(lane == h, e, 0.0), axis=-1, keepdims=True)
    _flash(q_ref, k_ref, v_ref, o_ref, ek_ref, eq, blk=blk)


def _flash_mla(q, kv, blk):
    b, s, _ = q.shape
    v_off = MLA_HEADS * QH // HEAD_DIM
    return pl.pallas_call(
        functools.partial(_flash_mla_kernel, blk=blk),
        grid=(b, MLA_HEADS),
        in_specs=[pl.BlockSpec((None, s, QH), lambda i, h: (i, 0, h)),
                  pl.BlockSpec((None, s, QH), lambda i, h: (i, 0, h)),
                  pl.BlockSpec((None, s, HEAD_DIM), lambda i, h: (i, 0, v_off + h))],
        out_specs=pl.BlockSpec((None, s, HEAD_DIM), lambda i, h: (i, 0, h)),
        out_shape=jax.ShapeDtypeStruct((b, s, MLA_HEADS * HEAD_DIM), F32),
        compiler_params=_params("arbitrary", "arbitrary"),
        name="flash_mla",
    )(q, kv, kv)


def _flash_fox(q, k, v, e, et, blk):
    b, s, _ = q.shape
    ek = et.reshape(b, FOX_HEADS, s // blk, blk)
    return pl.pallas_call(
        functools.partial(_flash_fox_kernel, blk=blk),
        grid=(b, FOX_HEADS),
        in_specs=[pl.BlockSpec((None, s, HEAD_DIM), lambda i, h: (i, 0, h)),
                  pl.BlockSpec((None, s, HEAD_DIM), lambda i, h: (i, 0, h // FOX_GROUP)),
                  pl.BlockSpec((None, s, HEAD_DIM), lambda i, h: (i, 0, h // FOX_GROUP)),
                  pl.BlockSpec((None, None, s // blk, blk), lambda i, h: (i, h, 0, 0)),
                  pl.BlockSpec((None, s, FOX_HEADS), lambda i, h: (i, 0, 0))],
        out_specs=pl.BlockSpec((None, s, HEAD_DIM), lambda i, h: (i, 0, h)),
        out_shape=jax.ShapeDtypeStruct((b, s, FOX_HEADS * HEAD_DIM), F32),
        compiler_params=_params("arbitrary", "arbitrary"),
        name="flash_fox",
    )(q, k, v, ek, e)


def _absorb_kernel(q_ref, w_ref, o_ref):
    o_ref[...] = lax.dot_general(q_ref[...], w_ref[...].astype(BF16),
                                 (((1,), (1,)), ((), ())),
                                 preferred_element_type=F32).astype(o_ref.dtype)


def _absorb(q256, w_uk2d):
    t = q256.shape[0]
    return pl.pallas_call(
        _absorb_kernel,
        grid=(MLA_HEADS,),
        in_specs=[pl.BlockSpec((t, HEAD_DIM), lambda h: (0, 2 * h)),
                  pl.BlockSpec((KV_LORA, HEAD_DIM), lambda h: (0, h))],
        out_specs=pl.BlockSpec((t, KV_LORA), lambda h: (0, h)),
        out_shape=jax.ShapeDtypeStruct((t, MLA_HEADS * KV_LORA), BF16),
        compiler_params=_params("arbitrary"),
        name="q_absorb",
    )(q256, w_uk2d)


def _value_up_kernel(o_ref_in, w_ref, o_ref):
    o_ref[...] = jnp.dot(o_ref_in[...].astype(BF16), w_ref[...].astype(BF16),
                         preferred_element_type=F32)


def _value_up(o_lat2d, w_uv2d):
    t = o_lat2d.shape[0]
    return pl.pallas_call(
        _value_up_kernel,
        grid=(MLA_HEADS,),
        in_specs=[pl.BlockSpec((t, KV_LORA), lambda h: (0, h)),
                  pl.BlockSpec((KV_LORA, HEAD_DIM), lambda h: (0, h))],
        out_specs=pl.BlockSpec((t, HEAD_DIM), lambda h: (0, h)),
        out_shape=jax.ShapeDtypeStruct((t, MLA_HEADS * HEAD_DIM), F32),
        compiler_params=_params("arbitrary"),
        name="value_up",
    )(o_lat2d, w_uv2d)


def _decode_kernel(pt_ref, qlat_ref, qrope_ref, fq_ref, ckvn_ref, krn_ref, fkn_ref, fvn_ref,
                   lfn_ref, lat_hbm, kr_hbm, fk_hbm, fv_hbm, lf_hbm,
                   olat_ref, ofox_ref,
                   lat_buf, kr_buf, fk_buf, fv_buf, lf_buf, sems,
                   m_m, l_m, acc_m, m_f, l_f, acc_f, carry, *, cp, nb, nch):
    b = pl.program_id(0)
    j = pl.program_id(1)
    total = nb * nch
    t = b * nch + j
    slot = t % DECODE_SLOTS
    kc = cp * PAGE_SIZE
    nt = (((1,), (1,)), ((), ()))
    bufs = (lat_buf, kr_buf, fk_buf, fv_buf, lf_buf)

    def start_chunk(step, sl):
        step = step % total
        bb = step // nch
        first_page = (nch - 1 - step % nch) * cp
        for p in range(cp):
            page = pt_ref[bb, first_page + p]
            dsts = (lat_buf.at[sl, p], kr_buf.at[sl, :, pl.ds(p * PAGE_SIZE, PAGE_SIZE)],
                    fk_buf.at[sl, p], fv_buf.at[sl, p], lf_buf.at[sl, p])
            for a, (hbm, dst) in enumerate(zip((lat_hbm, kr_hbm, fk_hbm, fv_hbm, lf_hbm), dsts)):
                pltpu.make_async_copy(hbm.at[page], dst, sems.at[a, sl]).start()

    def wait_chunk(sl):
        for a, buf in enumerate(bufs):
            pltpu.make_async_copy(buf.at[sl], buf.at[sl], sems.at[a, sl]).wait()

    @pl.when(t == 0)
    def _():
        for ahead in range(DECODE_SLOTS - 1):
            start_chunk(ahead, ahead)

    qlat = qlat_ref[...]
    qrope = qrope_ref[:, :ROPE_DIM]
    fq = fq_ref[...]
    head_col = lax.broadcasted_iota(jnp.int32, (FOX_HEADS, 1), 0)
    kv0 = head_col < FOX_GROUP

    @pl.when(j == 0)
    def _():
        ckvn = ckvn_ref[...].astype(BF16).astype(F32)
        krn = krn_ref[:, :ROPE_DIM].astype(BF16).astype(F32)
        m_m[...] = (jnp.sum(qlat.astype(F32) * ckvn, axis=-1, keepdims=True)
                    + jnp.sum(qrope.astype(F32) * krn, axis=-1, keepdims=True))
        l_m[...] = jnp.ones_like(l_m)
        acc_m[...] = jnp.broadcast_to(ckvn, acc_m.shape)
        fkn = fkn_ref[...].astype(BF16).astype(F32)
        fvn = fvn_ref[...].astype(BF16).astype(F32)
        fqf = fq.astype(F32)
        m_f[...] = jnp.where(kv0, jnp.sum(fqf * fkn[:, :HEAD_DIM], axis=-1, keepdims=True),
                             jnp.sum(fqf * fkn[:, HEAD_DIM:], axis=-1, keepdims=True))
        l_f[...] = jnp.ones_like(l_f)
        acc_f[...] = jnp.where(kv0, fvn[:, :HEAD_DIM], fvn[:, HEAD_DIM:])
        carry[...] = lfn_ref[...]

    wait_chunk(slot)
    start_chunk(t + DECODE_SLOTS - 1, (t + DECODE_SLOTS - 1) % DECODE_SLOTS)

    latb = lat_buf[slot].reshape(kc, KV_LORA).astype(BF16)
    krb = kr_buf[slot].astype(BF16)
    s = (lax.dot_general(qlat, latb, nt, preferred_element_type=F32)
         + jnp.dot(qrope, krb, preferred_element_type=F32))
    m_old = m_m[...]
    m_new = jnp.maximum(m_old, jnp.max(s, axis=-1, keepdims=True))
    alpha = jnp.exp(m_old - m_new)
    p = jnp.exp(s - m_new)
    l_m[...] = alpha * l_m[...] + jnp.sum(p, axis=-1, keepdims=True)
    acc_m[...] = alpha * acc_m[...] + jnp.dot(p.astype(BF16), latb, preferred_element_type=F32)
    m_m[...] = m_new

    lf = lf_buf[slot]
    x1, x2, x3 = _split3(lf.reshape(cp * FOX_HEADS, PAGE_SIZE))
    jr = lax.broadcasted_iota(jnp.int32, (PAGE_SIZE, PAGE_SIZE), 0)
    kcol = lax.broadcasted_iota(jnp.int32, (PAGE_SIZE, PAGE_SIZE), 1)
    later = (jr > kcol).astype(BF16)
    within = (jnp.dot(x1, later, preferred_element_type=F32)
              + jnp.dot(x2, later, preferred_element_type=F32)
              + jnp.dot(x3, later, preferred_element_type=F32))
    off = carry[...]
    tiles = [None] * cp
    for pg in range(cp - 1, -1, -1):
        tiles[pg] = within[pg * FOX_HEADS:(pg + 1) * FOX_HEADS, :] + off
        off = off + jnp.sum(lf[pg], axis=-1, keepdims=True)
    carry[...] = off
    bias = jnp.concatenate(tiles, axis=1)

    def kv_head(buf, n):
        rows = buf[slot, :, pl.ds(n, PAGE_SIZE, stride=FOX_KV_HEADS), :]
        return rows.reshape(kc, HEAD_DIM).astype(BF16)

    s = jnp.where(kv0, lax.dot_general(fq, kv_head(fk_buf, 0), nt, preferred_element_type=F32),
                  lax.dot_general(fq, kv_head(fk_buf, 1), nt, preferred_element_type=F32)) + bias
    m_old = m_f[...]
    m_new = jnp.maximum(m_old, jnp.max(s, axis=-1, keepdims=True))
    alpha = jnp.exp(m_old - m_new)
    p = jnp.exp(s - m_new)
    pb = p.astype(BF16)
    l_f[...] = alpha * l_f[...] + jnp.sum(p, axis=-1, keepdims=True)
    pv = jnp.where(kv0, jnp.dot(pb, kv_head(fv_buf, 0), preferred_element_type=F32),
                   jnp.dot(pb, kv_head(fv_buf, 1), preferred_element_type=F32))
    acc_f[...] = alpha * acc_f[...] + pv
    m_f[...] = m_new

    @pl.when(j == nch - 1)
    def _():
        olat_ref[...] = acc_m[...] / l_m[...]
        ofox_ref[...] = acc_f[...] / l_f[...]

    @pl.when(t == total - 1)
    def _():
        for ahead in range(1, DECODE_SLOTS):
            wait_chunk((t + ahead) % DECODE_SLOTS)


def _decode_attention(page_table, qlat, qrope, fq, ckv_new, kr_new, fk_new, fv_new, lf_new,
                      lat_c, kr_c, fk_c, fv_c, lf_c):
    nb, n_pages = page_table.shape
    cp = DECODE_PAGES
    ns = DECODE_SLOTS
    kvw = FOX_KV_HEADS * HEAD_DIM
    kv_rows = FOX_KV_HEADS * PAGE_SIZE
    per_b = lambda *shape: pl.BlockSpec((None,) + shape, lambda b, j, pt: (b, 0, 0))
    hbm = pl.BlockSpec(memory_space=pl.ANY)
    grid_spec = pltpu.PrefetchScalarGridSpec(
        num_scalar_prefetch=1,
        grid=(nb, n_pages // cp),
        in_specs=[per_b(MLA_HEADS, KV_LORA), per_b(MLA_HEADS, LANES), per_b(FOX_HEADS, HEAD_DIM),
                  per_b(1, KV_LORA), per_b(1, LANES), per_b(1, kvw), per_b(1, kvw),
                  per_b(FOX_HEADS, 1), hbm, hbm, hbm, hbm, hbm],
        out_specs=[per_b(MLA_HEADS, KV_LORA), per_b(FOX_HEADS, HEAD_DIM)],
        scratch_shapes=[
            pltpu.VMEM((ns, cp, PAGE_SIZE, KV_LORA), F32),
            pltpu.VMEM((ns, ROPE_DIM, cp * PAGE_SIZE), F32),
            pltpu.VMEM((ns, cp, kv_rows, HEAD_DIM), F32),
            pltpu.VMEM((ns, cp, kv_rows, HEAD_DIM), F32),
            pltpu.VMEM((ns, cp, FOX_HEADS, PAGE_SIZE), F32),
            pltpu.SemaphoreType.DMA((5, ns)),
            pltpu.VMEM((MLA_HEADS, 1), F32), pltpu.VMEM((MLA_HEADS, 1), F32),
            pltpu.VMEM((MLA_HEADS, KV_LORA), F32),
            pltpu.VMEM((FOX_HEADS, 1), F32), pltpu.VMEM((FOX_HEADS, 1), F32),
            pltpu.VMEM((FOX_HEADS, HEAD_DIM), F32),
            pltpu.VMEM((FOX_HEADS, 1), F32),
        ],
    )
    return pl.pallas_call(
        functools.partial(_decode_kernel, cp=cp, nb=nb, nch=n_pages // cp),
        grid_spec=grid_spec,
        out_shape=[jax.ShapeDtypeStruct((nb, MLA_HEADS, KV_LORA), F32),
                   jax.ShapeDtypeStruct((nb, FOX_HEADS, HEAD_DIM), F32)],
        compiler_params=_params("arbitrary", "arbitrary"),
        name="decode_attention",
    )(page_table, qlat, qrope, fq, ckv_new, kr_new, fk_new, fv_new, lf_new,
      lat_c, kr_c, fk_c, fv_c, lf_c)


def _swap_halves(x):
    half = x.shape[-1] // 2
    return jnp.concatenate([x[..., half:], x[..., :half]], axis=-1)


def _relayout_w_in(w):
    o = 0
    segs = {}
    for name, width in (("cq", Q_LORA), ("ckv", KV_LORA), ("kr", ROPE_DIM),
                        ("fq", FOX_HEADS * HEAD_DIM), ("fk", FOX_KV_HEADS * HEAD_DIM),
                        ("fv", FOX_KV_HEADS * HEAD_DIM), ("ff", FOX_HEADS)):
        segs[name] = w[:, o:o + width]
        o += width
    zeros = lambda n: jnp.zeros((w.shape[0], n), w.dtype)
    cols = [segs["cq"], segs["ckv"], segs["fq"], segs["fk"], segs["fv"],
            segs["kr"], _swap_halves(segs["kr"]), segs["ff"], zeros(LANES - FOX_HEADS),
            zeros(Z_WIDTH - Z_FF - LANES)]
    return jnp.concatenate(cols, axis=1).astype(BF16)


def _relayout_w_q_up(w):
    nope, rope = w[..., :HEAD_DIM], w[..., HEAD_DIM:]
    ext = jnp.concatenate([nope, rope, _swap_halves(rope)], axis=-1)
    return ext.reshape(w.shape[0], MLA_HEADS * QH).astype(BF16)


def _build_w_kv(w_uk, w_uv):
    wk = jnp.zeros((KVA_WIDTH, MLA_HEADS, QH), F32)
    wk = wk.at[:KV_LORA, :, :HEAD_DIM].set(w_uk)
    eye = jnp.broadcast_to(jnp.eye(ROPE_DIM, dtype=F32)[:, None, :], (ROPE_DIM, MLA_HEADS, ROPE_DIM))
    wk = wk.at[KV_LORA:KV_LORA + ROPE_DIM, :, HEAD_DIM:HEAD_DIM + ROPE_DIM].set(eye)
    wv = jnp.zeros((KVA_WIDTH, MLA_HEADS, HEAD_DIM), F32).at[:KV_LORA].set(w_uv)
    return jnp.concatenate([wk.reshape(KVA_WIDTH, -1), wv.reshape(KVA_WIDTH, -1)], axis=1).astype(BF16)


def _rope_tables(pos):
    half = ROPE_DIM // 2
    inv_freq = jnp.power(ROPE_THETA, -jnp.arange(half, dtype=F32) * 2.0 / ROPE_DIM)
    ang = pos.astype(F32)[:, None] * inv_freq[None, :]
    cos, sin = jnp.cos(ang), jnp.sin(ang)
    z = jnp.zeros((pos.shape[0], LANES - ROPE_DIM), F32)
    return (jnp.concatenate([cos, cos, z], axis=1), jnp.concatenate([-sin, sin, z], axis=1))


def _layer(x, mods, rows_per_batch, pos_tables, pos_blocks, tm, w, attend):
    sh1, sc1, ga1, sh2, sc2, ga2 = [_Mod(m, rows_per_batch) for m in mods]
    cs1, cs2 = pos_tables
    tm_small = min(tm, 256)
    h = _norm_mod(x, w["g_attn"], sc1, sh1, tm_small)
    z = _dense(h, w["w_in"], tm=tm, tn=512, out_dtype=F32, name="in_proj")
    (cqn, lat, kva, kr, fq, fk, fv, fkb, fvb, lf, lf128) = _post_in(
        z, w["g_q"], w["g_kv"], w["b_f"], cs1, cs2, tm_small, pos_blocks * (tm // tm_small) if pos_blocks > 1 else 1)
    pos_spec = pl.BlockSpec((tm, LANES), lambda i, j: (i % pos_blocks, 0))
    q256 = _dense(cqn, w["w_q"], tm=tm, tn=512, out_dtype=BF16, name="q_up",
                  epi=_epi_q_rope, extras=((cs1, pos_spec), (cs2, pos_spec)))
    ym, yf = attend(q256, kva, fq, fkb, fvb, lf128, lat, kr, fk, fv, lf)
    y = _norm2(ym, yf, w["g_mla"], w["g_fox"], tm_small)
    x = _dense(y, w["w_out"], tm=tm, tn=512, out_dtype=F32, name="out_proj", epi=_epi_residual,
               extras=((x, pl.BlockSpec((tm, 512), lambda i, j: (i, j))), (ga1.arr, ga1.spec(tm, 512))))
    h = _norm_mod(x, w["g_ffn"], sc2, sh2, tm_small)
    a = _gate_up(h, w["w_gate"], w["w_up"], tm, 256)
    tmd = min(tm, 512)
    x = _dense(a, w["w_down"], tm=tmd, tn=256, out_dtype=F32, name="ffn_down", epi=_epi_residual,
               extras=((x, pl.BlockSpec((tmd, 256), lambda i, j: (i, j))), (ga2.arr, ga2.spec(tmd, 256))))
    return x, (lat, kr, fk, fv, lf)


def kernel(x_prompt, x_sample, c_prompt, c_sample, cache_mla_latent, cache_mla_krope, cache_fox_k,
           cache_fox_v, cache_fox_logf, page_table, w_ada, b_ada, g_attn_norm, w_in, g_q_norm, w_q_up,
           g_kv_norm, w_uk, w_uv, b_forget, g_mla_out, g_fox_out, w_out, g_ffn_norm, w_gate, w_up,
           w_down, g_final):
    nbp, seq, d = x_prompt.shape
    nbs = x_sample.shape[0]
    depth = w_ada.shape[0]
    n_phys = cache_mla_latent.shape[1]
    past = page_table.shape[1] * PAGE_SIZE
    kvw = FOX_KV_HEADS * HEAD_DIM

    xp = x_prompt.reshape(nbp * seq, d)
    xs = x_sample.reshape(nbs, d)
    c_all = jnp.concatenate([c_prompt, c_sample], axis=0)
    tables_p = _rope_tables(jnp.arange(seq, dtype=jnp.int32))
    tables_s = tuple(jnp.broadcast_to(t, (nbs, LANES))
                     for t in _rope_tables(past + jnp.arange(1, dtype=jnp.int32)))
    tm_p = 1024
    st_p, st_s = [], []
    for l in range(depth):
        w = dict(g_attn=g_attn_norm[l], w_in=_relayout_w_in(w_in[l]), g_q=g_q_norm[l],
                 w_q=_relayout_w_q_up(w_q_up[l]), g_kv=g_kv_norm[l], b_f=b_forget[l],
                 g_mla=g_mla_out[l], g_fox=g_fox_out[l], w_out=w_out[l], g_ffn=g_ffn_norm[l],
                 w_gate=w_gate[l], w_up=w_up[l], w_down=w_down[l].astype(BF16))
        w_kv = _build_w_kv(w_uk[l], w_uv[l])
        w_uk2d = w_uk[l].reshape(KV_LORA, MLA_HEADS * HEAD_DIM)
        w_uv2d = w_uv[l].reshape(KV_LORA, MLA_HEADS * HEAD_DIM)
        mod = _ada(c_all, w_ada[l], b_ada[l])
        mods_p = [mod[:nbp, i * d:(i + 1) * d] for i in range(N_MOD)]
        mods_s = [mod[nbp:, i * d:(i + 1) * d] for i in range(N_MOD)]

        def attend_prompt(q256, kva, fq, fkb, fvb, lf128, *_):
            kv = _dense(kva, w_kv, tm=tm_p, tn=512, out_dtype=BF16, name="kv_up")
            ym = _flash_mla(q256.reshape(nbp, seq, -1), kv.reshape(nbp, seq, -1), 512)
            e, et = _suffix(lf128.reshape(nbp, seq, LANES))
            yf = _flash_fox(fq.reshape(nbp, seq, -1), fkb.reshape(nbp, seq, kvw),
                            fvb.reshape(nbp, seq, kvw), e, et, 512)
            return ym.reshape(nbp * seq, -1), yf.reshape(nbp * seq, -1)

        def attend_sample(q256, kva, fq, fkb, fvb, lf128, lat, kr, fk, fv, lf):
            qlat = _absorb(q256, w_uk2d).reshape(nbs, MLA_HEADS, KV_LORA)
            qrope = q256.reshape(nbs, MLA_HEADS, QH)[:, :, LANES:]
            o_lat, o_fox = _decode_attention(
                page_table, qlat, qrope, fq.reshape(nbs, FOX_HEADS, HEAD_DIM),
                lat.reshape(nbs, 1, KV_LORA), kva[:, KV_LORA:].astype(F32).reshape(nbs, 1, LANES),
                fk.reshape(nbs, 1, kvw), fv.reshape(nbs, 1, kvw), lf.reshape(nbs, FOX_HEADS, 1),
                cache_mla_latent[l], jnp.swapaxes(cache_mla_krope[l], 1, 2),
                cache_fox_k[l].reshape(n_phys, PAGE_SIZE * FOX_KV_HEADS, HEAD_DIM),
                cache_fox_v[l].reshape(n_phys, PAGE_SIZE * FOX_KV_HEADS, HEAD_DIM),
                jnp.swapaxes(cache_fox_logf[l], 1, 2))
            ym = _value_up(o_lat.reshape(nbs, MLA_HEADS * KV_LORA), w_uv2d)
            return ym, o_fox.reshape(nbs, FOX_HEADS * HEAD_DIM)

        xp, sp = _layer(xp, mods_p, seq, tables_p, seq // tm_p, tm_p, w, attend_prompt)
        xs, ss = _layer(xs, mods_s, 1, tables_s, 1, nbs, w, attend_sample)
        st_p.append(sp)
        st_s.append(ss)

    y_prompt = _norm(xp, g_final, 256).reshape(nbp, seq, d)
    y_sample = _norm(xs, g_final, nbs).reshape(nbs, 1, d)

    def stack(states, idx, shape):
        return jnp.stack([s[idx].reshape(shape) for s in states])

    outs = [y_prompt, y_sample]
    for states, lead in ((st_p, (nbp, seq)), (st_s, (nbs, 1))):
        outs += [stack(states, 0, lead + (KV_LORA,)), stack(states, 1, lead + (ROPE_DIM,)),
                 stack(states, 2, lead + (FOX_KV_HEADS, HEAD_DIM)),
                 stack(states, 3, lead + (FOX_KV_HEADS, HEAD_DIM)),
                 stack(states, 4, lead + (FOX_HEADS,))]
    return tuple(outs)
```

```python
import functools

import jax
import jax.numpy as jnp
from jax import lax
from jax.experimental import pallas as pl
from jax.experimental.pallas import tpu as pltpu

F32 = jnp.float32
BF16 = jnp.bfloat16

D_MODEL = 4096
HEAD_DIM = 128
MLA_HEADS = 16
FOX_HEADS = 16
FOX_KV_HEADS = 2
FOX_GROUP = FOX_HEADS // FOX_KV_HEADS
Q_LORA = 1536
KV_LORA = 512
ROPE_DIM = 64
ROPE_THETA = 10000.0
MLA_SCALE = (HEAD_DIM + ROPE_DIM) ** -0.5
FOX_SCALE = HEAD_DIM ** -0.5
N_MOD = 6
PAGE_SIZE = 128
EPS = 1e-6

LANES = 128
VMEM_LIMIT = 56 * 1024 * 1024

Z_CQ = 0
Z_CKV = Z_CQ + Q_LORA
Z_FQ = Z_CKV + KV_LORA
Z_FK = Z_FQ + FOX_HEADS * HEAD_DIM
Z_FV = Z_FK + FOX_KV_HEADS * HEAD_DIM
Z_KR = Z_FV + FOX_KV_HEADS * HEAD_DIM
Z_FF = Z_KR + LANES
Z_WIDTH = 5120
KVA_WIDTH = KV_LORA + LANES
QH = 2 * LANES
FLASH_BLOCK = 1024
DECODE_PAGES = 16
DECODE_SLOTS = 3


def _params(*sem):
    return pltpu.CompilerParams(dimension_semantics=sem, vmem_limit_bytes=VMEM_LIMIT)


def _rms(x):
    return x * lax.rsqrt(jnp.mean(x * x, axis=-1, keepdims=True) + EPS)


def _rope_lanes(g, cs1, cs2):
    return g * cs1 + pltpu.roll(g, ROPE_DIM, axis=1) * cs2


def _ada_kernel(c_ref, w_ref, b_ref, o_ref):
    c = c_ref[...]
    a = (c * jax.nn.sigmoid(c)).astype(BF16)
    o_ref[...] = jnp.dot(a, w_ref[...].astype(BF16), preferred_element_type=F32) + b_ref[...]


def _ada(c, w, b):
    m, k = c.shape
    n = w.shape[1]
    tn = 512
    return pl.pallas_call(
        _ada_kernel,
        grid=(n // tn,),
        in_specs=[pl.BlockSpec((m, k), lambda j: (0, 0)),
                  pl.BlockSpec((k, tn), lambda j: (0, j)),
                  pl.BlockSpec((1, tn), lambda j: (0, j))],
        out_specs=pl.BlockSpec((m, tn), lambda j: (0, j)),
        out_shape=jax.ShapeDtypeStruct((m, n), F32),
        compiler_params=_params("arbitrary"),
        name="ada",
    )(c, w, b.reshape(1, n))


class _Mod:
    def __init__(self, arr, rows_per_batch):
        self.rows_per_batch = rows_per_batch
        if rows_per_batch == 1:
            self.arr = arr
        else:
            self.arr = arr.reshape(arr.shape[0], 1, arr.shape[1])

    def spec(self, tm, tn, row_only=False):
        if self.rows_per_batch == 1:
            if row_only:
                return pl.BlockSpec((tm, tn), lambda i: (i, 0))
            return pl.BlockSpec((tm, tn), lambda i, j: (i, j))
        per = self.rows_per_batch // tm
        if row_only:
            return pl.BlockSpec((None, 1, tn), lambda i: (i // per, 0, 0))
        return pl.BlockSpec((None, 1, tn), lambda i, j: (i // per, 0, j))


def _norm_mod_kernel(x_ref, g_ref, sc_ref, sh_ref, o_ref):
    y = _rms(x_ref[...]) * g_ref[...]
    o_ref[...] = (y * (1.0 + sc_ref[...]) + sh_ref[...]).astype(o_ref.dtype)


def _norm_kernel(x_ref, g_ref, o_ref):
    o_ref[...] = (_rms(x_ref[...]) * g_ref[...]).astype(o_ref.dtype)


def _norm_mod(x, g, sc, sh, tm):
    t, d = x.shape
    return pl.pallas_call(
        _norm_mod_kernel,
        grid=(t // tm,),
        in_specs=[pl.BlockSpec((tm, d), lambda i: (i, 0)),
                  pl.BlockSpec((1, d), lambda i: (0, 0)),
                  sc.spec(tm, d, row_only=True),
                  sh.spec(tm, d, row_only=True)],
        out_specs=pl.BlockSpec((tm, d), lambda i: (i, 0)),
        out_shape=jax.ShapeDtypeStruct((t, d), BF16),
        compiler_params=_params("arbitrary"),
        name="norm_mod",
    )(x, g.reshape(1, d), sc.arr, sh.arr)


def _norm(x, g, tm):
    t, d = x.shape
    return pl.pallas_call(
        _norm_kernel,
        grid=(t // tm,),
        in_specs=[pl.BlockSpec((tm, d), lambda i: (i, 0)),
                  pl.BlockSpec((1, d), lambda i: (0, 0))],
        out_specs=pl.BlockSpec((tm, d), lambda i: (i, 0)),
        out_shape=jax.ShapeDtypeStruct((t, d), F32),
        compiler_params=_params("arbitrary"),
        name="final_norm",
    )(x, g.reshape(1, d))


def _norm2_kernel(ym_ref, yf_ref, gm_ref, gf_ref, o_ref):
    w = ym_ref.shape[1]
    o_ref[:, :w] = (_rms(ym_ref[...]) * gm_ref[...]).astype(o_ref.dtype)
    o_ref[:, w:] = (_rms(yf_ref[...]) * gf_ref[...]).astype(o_ref.dtype)


def _norm2(ym, yf, gm, gf, tm):
    t, w = ym.shape
    return pl.pallas_call(
        _norm2_kernel,
        grid=(t // tm,),
        in_specs=[pl.BlockSpec((tm, w), lambda i: (i, 0)),
                  pl.BlockSpec((tm, w), lambda i: (i, 0)),
                  pl.BlockSpec((1, w), lambda i: (0, 0)),
                  pl.BlockSpec((1, w), lambda i: (0, 0))],
        out_specs=pl.BlockSpec((tm, 2 * w), lambda i: (i, 0)),
        out_shape=jax.ShapeDtypeStruct((t, 2 * w), BF16),
        compiler_params=_params("arbitrary"),
        name="mix_norm",
    )(ym, yf, gm.reshape(1, w), gf.reshape(1, w))


def _mm_kernel(a_ref, w_ref, *rest, epi, n_extra):
    o_ref = rest[n_extra]
    acc = jnp.dot(a_ref[...].astype(BF16), w_ref[...].astype(BF16),
                  preferred_element_type=F32)
    if epi is not None:
        acc = epi(acc, *rest[:n_extra])
    o_ref[...] = acc.astype(o_ref.dtype)


def _dense(a, w, *, tm, tn, out_dtype, name, epi=None, extras=()):
    m, k = a.shape
    n = w.shape[1]
    assert m % tm == 0 and n % tn == 0
    return pl.pallas_call(
        functools.partial(_mm_kernel, epi=epi, n_extra=len(extras)),
        grid=(m // tm, n // tn),
        in_specs=[pl.BlockSpec((tm, k), lambda i, j: (i, 0)),
                  pl.BlockSpec((k, tn), lambda i, j: (0, j))] + [s for _, s in extras],
        out_specs=pl.BlockSpec((tm, tn), lambda i, j: (i, j)),
        out_shape=jax.ShapeDtypeStruct((m, n), out_dtype),
        compiler_params=_params("arbitrary", "arbitrary"),
        name=name,
    )(a, w, *[x for x, _ in extras])


def _epi_residual(acc, x_ref, ga_ref):
    return x_ref[...] + ga_ref[...] * acc


def _epi_q_rope(acc, cs1_ref, cs2_ref):
    cs1 = cs1_ref[...]
    cs2 = cs2_ref[...]
    parts = []
    for h in range(acc.shape[1] // QH):
        parts.append(acc[:, h * QH:h * QH + LANES])
        parts.append(_rope_lanes(acc[:, h * QH + LANES:(h + 1) * QH], cs1, cs2))
    return jnp.concatenate(parts, axis=1) * MLA_SCALE


def _gate_up_kernel(a_ref, wg_ref, wu_ref, o_ref):
    a = a_ref[...]
    g = jnp.dot(a, wg_ref[...].astype(BF16), preferred_element_type=F32)
    u = jnp.dot(a, wu_ref[...].astype(BF16), preferred_element_type=F32)
    o_ref[...] = (g * jax.nn.sigmoid(g) * u).astype(o_ref.dtype)


def _gate_up(a, wg, wu, tm, tn):
    m, k = a.shape
    n = wg.shape[1]
    assert m % tm == 0 and n % tn == 0
    return pl.pallas_call(
        _gate_up_kernel,
        grid=(m // tm, n // tn),
        in_specs=[pl.BlockSpec((tm, k), lambda i, j: (i, 0)),
                  pl.BlockSpec((k, tn), lambda i, j: (0, j)),
                  pl.BlockSpec((k, tn), lambda i, j: (0, j))],
        out_specs=pl.BlockSpec((tm, tn), lambda i, j: (i, j)),
        out_shape=jax.ShapeDtypeStruct((m, n), BF16),
        compiler_params=_params("arbitrary", "arbitrary"),
        name="ffn_gate_up",
    )(a, wg, wu)


def _post_in_kernel(z_ref, gq_ref, gkv_ref, bf_ref, cs1_ref, cs2_ref,
                    cqn_ref, lat_ref, kva_ref, kr_ref, fq_ref, fk_ref, fv_ref,
                    fkb_ref, fvb_ref, lf_ref, lf128_ref):
    cqn_ref[...] = (_rms(z_ref[:, Z_CQ:Z_CKV]) * gq_ref[...]).astype(BF16)
    lat = _rms(z_ref[:, Z_CKV:Z_FQ]) * gkv_ref[...]
    lat_ref[...] = lat
    kr = _rope_lanes(z_ref[:, Z_KR:Z_FF], cs1_ref[...], cs2_ref[...])
    kr_ref[...] = kr[:, :ROPE_DIM]
    kva_ref[:, :KV_LORA] = lat.astype(BF16)
    kva_ref[:, KV_LORA:] = kr.astype(BF16)
    fq_ref[...] = (z_ref[:, Z_FQ:Z_FK] * FOX_SCALE).astype(BF16)
    fk = z_ref[:, Z_FK:Z_FV]
    fv = z_ref[:, Z_FV:Z_KR]
    fk_ref[...] = fk
    fv_ref[...] = fv
    fkb_ref[...] = fk.astype(BF16)
    fvb_ref[...] = fv.astype(BF16)
    x = z_ref[:, Z_FF:Z_FF + LANES] + bf_ref[...]
    lf = jnp.minimum(x, 0.0) - jnp.log1p(jnp.exp(-jnp.abs(x)))
    lane = lax.broadcasted_iota(jnp.int32, lf.shape, 1)
    lf = jnp.where(lane < FOX_HEADS, lf, 0.0)
    lf128_ref[...] = lf
    lf_ref[...] = lf[:, :FOX_HEADS]


def _post_in(z, g_q, g_kv, b_f, cs1, cs2, tm, pos_blocks):
    t = z.shape[0]
    kvw = FOX_KV_HEADS * HEAD_DIM
    row = lambda w: pl.BlockSpec((tm, w), lambda i: (i, 0))
    const = lambda w: pl.BlockSpec((1, w), lambda i: (0, 0))
    pos = pl.BlockSpec((tm, LANES), lambda i: (i % pos_blocks, 0))
    bf = jnp.zeros((1, LANES), F32).at[0, :FOX_HEADS].set(b_f)
    shapes = [((t, Q_LORA), BF16), ((t, KV_LORA), F32), ((t, KVA_WIDTH), BF16),
              ((t, ROPE_DIM), F32), ((t, FOX_HEADS * HEAD_DIM), BF16),
              ((t, kvw), F32), ((t, kvw), F32), ((t, kvw), BF16), ((t, kvw), BF16),
              ((t, FOX_HEADS), F32), ((t, LANES), F32)]
    return pl.pallas_call(
        _post_in_kernel,
        grid=(t // tm,),
        in_specs=[row(Z_WIDTH), const(Q_LORA), const(KV_LORA), const(LANES), pos, pos],
        out_specs=[row(s[1]) for s, _ in shapes],
        out_shape=[jax.ShapeDtypeStruct(s, dt) for s, dt in shapes],
        compiler_params=_params("arbitrary"),
        name="post_in",
    )(z, g_q.reshape(1, Q_LORA), g_kv.reshape(1, KV_LORA), bf, cs1, cs2)


def _split3(x):
    a1 = x.astype(BF16)
    r1 = x - a1.astype(F32)
    a2 = r1.astype(BF16)
    a3 = (r1 - a2.astype(F32)).astype(BF16)
    return a1, a2, a3


def _suffix_kernel(lf_ref, e_ref, et_ref, *, rows):
    s = lf_ref.shape[0]
    parts = _split3(lf_ref[...])
    blocks = []
    for r0 in range(0, s, rows):
        row = lax.broadcasted_iota(jnp.int32, (rows, s), 0) + r0
        col = lax.broadcasted_iota(jnp.int32, (rows, s), 1)
        upper = (col > row).astype(BF16)
        blocks.append(sum(jnp.dot(upper, a, preferred_element_type=F32) for a in parts))
    e = jnp.concatenate(blocks, axis=0)
    e_ref[...] = e[:, :FOX_HEADS]
    et_ref[...] = e.T[:FOX_HEADS, :]


def _suffix(lf128):
    b, s, _ = lf128.shape
    return pl.pallas_call(
        functools.partial(_suffix_kernel, rows=512),
        grid=(b,),
        in_specs=[pl.BlockSpec((None, s, LANES), lambda i: (i, 0, 0))],
        out_specs=[pl.BlockSpec((None, s, FOX_HEADS), lambda i: (i, 0, 0)),
                   pl.BlockSpec((None, FOX_HEADS, s), lambda i: (i, 0, 0))],
        out_shape=[jax.ShapeDtypeStruct((b, s, FOX_HEADS), F32),
                   jax.ShapeDtypeStruct((b, FOX_HEADS, s), F32)],
        compiler_params=_params("arbitrary"),
        name="suffix_logf",
    )(lf128)


def _flash(q_ref, k_ref, v_ref, o_ref, ek_ref, eq, *, blk):
    s_len = q_ref.shape[0]
    dv = v_ref.shape[1]
    row = lax.broadcasted_iota(jnp.int32, (blk, blk), 0)
    col = lax.broadcasted_iota(jnp.int32, (blk, blk), 1)
    causal = row >= col
    for qi in range(s_len // blk):
        q = q_ref[qi * blk:(qi + 1) * blk, :]
        eq_blk = None if eq is None else eq[qi * blk:(qi + 1) * blk, :]

        def step(ki, carry, diagonal):
            m, l, acc = carry
            k0 = ki * blk if diagonal else pl.multiple_of(ki * blk, blk)
            k = k_ref[pl.ds(k0, blk), :]
            v = v_ref[pl.ds(k0, blk), :]
            s = lax.dot_general(q, k, (((1,), (1,)), ((), ())), preferred_element_type=F32)
            if ek_ref is not None:
                s = s + ek_ref[pl.ds(ki, 1), :] - eq_blk
            if diagonal:
                s = jnp.where(causal, s, -jnp.inf)
            m_new = jnp.maximum(m, jnp.max(s, axis=-1, keepdims=True))
            alpha = jnp.exp(m - m_new)
            p = jnp.exp(s - m_new)
            l = alpha * l + jnp.sum(p, axis=-1, keepdims=True)
            acc = alpha * acc + jnp.dot(p.astype(BF16), v, preferred_element_type=F32)
            return m_new, l, acc

        carry = (jnp.full((blk, 1), -jnp.inf, F32), jnp.zeros((blk, 1), F32),
                 jnp.zeros((blk, dv), F32))
        if qi > 0:
            carry = lax.fori_loop(0, qi, lambda ki, c: step(ki, c, False), carry)
        _, l, acc = step(qi, carry, True)
        o_ref[qi * blk:(qi + 1) * blk, :] = acc / l


def _flash_mla_kernel(q_ref, k_ref, v_ref, o_ref, *, blk):
    _flash(q_ref, k_ref, v_ref, o_ref, None, None, blk=blk)


def _flash_fox_kernel(q_ref, k_ref, v_ref, ek_ref, e_ref, o_ref, *, blk):
    h = pl.program_id(1)
    e = e_ref[...]
    lane = lax.broadcasted_iota(jnp.int32, e.shape, 1)
    eq = jnp.sum(jnp.where(lane == h, e, 0.0), axis=-1, keepdims=True)
    _flash(q_ref, k_ref, v_ref, o_ref, ek_ref, eq, blk=blk)


def _flash_mla(q, kv, blk):
    b, s, _ = q.shape
    v_off = MLA_HEADS * QH // HEAD_DIM
    return pl.pallas_call(
        functools.partial(_flash_mla_kernel, blk=blk),
        grid=(b, MLA_HEADS),
        in_specs=[pl.BlockSpec((None, s, QH), lambda i, h: (i, 0, h)),
                  pl.BlockSpec((None, s, QH), lambda i, h: (i, 0, h)),
                  pl.BlockSpec((None, s, HEAD_DIM), lambda i, h: (i, 0, v_off + h))],
        out_specs=pl.BlockSpec((None, s, HEAD_DIM), lambda i, h: (i, 0, h)),
        out_shape=jax.ShapeDtypeStruct((b, s, MLA_HEADS * HEAD_DIM), F32),
        compiler_params=_params("arbitrary", "arbitrary"),
        name="flash_mla",
    )(q, kv, kv)


def _flash_fox(q, k, v, e, et, blk):
    b, s, _ = q.shape
    ek = et.reshape(b, FOX_HEADS, s // blk, blk)
    return pl.pallas_call(
        functools.partial(_flash_fox_kernel, blk=blk),
        grid=(b, FOX_HEADS),
        in_specs=[pl.BlockSpec((None, s, HEAD_DIM), lambda i, h: (i, 0, h)),
                  pl.BlockSpec((None, s, HEAD_DIM), lambda i, h: (i, 0, h // FOX_GROUP)),
                  pl.BlockSpec((None, s, HEAD_DIM), lambda i, h: (i, 0, h // FOX_GROUP)),
                  pl.BlockSpec((None, None, s // blk, blk), lambda i, h: (i, h, 0, 0)),
                  pl.BlockSpec((None, s, FOX_HEADS), lambda i, h: (i, 0, 0))],
        out_specs=pl.BlockSpec((None, s, HEAD_DIM), lambda i, h: (i, 0, h)),
        out_shape=jax.ShapeDtypeStruct((b, s, FOX_HEADS * HEAD_DIM), F32),
        compiler_params=_params("arbitrary", "arbitrary"),
        name="flash_fox",
    )(q, k, v, ek, e)


def _absorb_kernel(q_ref, w_ref, o_ref):
    o_ref[...] = lax.dot_general(q_ref[...], w_ref[...].astype(BF16),
                                 (((1,), (1,)), ((), ())),
                                 preferred_element_type=F32).astype(o_ref.dtype)


def _absorb(q256, w_uk2d):
    t = q256.shape[0]
    return pl.pallas_call(
        _absorb_kernel,
        grid=(MLA_HEADS,),
        in_specs=[pl.BlockSpec((t, HEAD_DIM), lambda h: (0, 2 * h)),
                  pl.BlockSpec((KV_LORA, HEAD_DIM), lambda h: (0, h))],
        out_specs=pl.BlockSpec((t, KV_LORA), lambda h: (0, h)),
        out_shape=jax.ShapeDtypeStruct((t, MLA_HEADS * KV_LORA), BF16),
        compiler_params=_params("arbitrary"),
        name="q_absorb",
    )(q256, w_uk2d)


def _value_up_kernel(o_ref_in, w_ref, o_ref):
    o_ref[...] = jnp.dot(o_ref_in[...].astype(BF16), w_ref[...].astype(BF16),
                         preferred_element_type=F32)


def _value_up(o_lat2d, w_uv2d):
    t = o_lat2d.shape[0]
    return pl.pallas_call(
        _value_up_kernel,
        grid=(MLA_HEADS,),
        in_specs=[pl.BlockSpec((t, KV_LORA), lambda h: (0, h)),
                  pl.BlockSpec((KV_LORA, HEAD_DIM), lambda h: (0, h))],
        out_specs=pl.BlockSpec((t, HEAD_DIM), lambda h: (0, h)),
        out_shape=jax.ShapeDtypeStruct((t, MLA_HEADS * HEAD_DIM), F32),
        compiler_params=_params("arbitrary"),
        name="value_up",
    )(o_lat2d, w_uv2d)


def _decode_kernel(pt_ref, qlat_ref, qrope_ref, fq_ref, ckvn_ref, krn_ref, fkn_ref, fvn_ref,
                   lfn_ref, lat_hbm, kr_hbm, fk_hbm, fv_hbm, lf_hbm,
                   olat_ref, ofox_ref,
                   lat_buf, kr_buf, fk_buf, fv_buf, lf_buf, sems,
                   m_m, l_m, acc_m, m_f, l_f, acc_f, carry, *, cp, nb, nch):
    b = pl.program_id(0)
    j = pl.program_id(1)
    total = nb * nch
    t = b * nch + j
    slot = t % DECODE_SLOTS
    kc = cp * PAGE_SIZE
    nt = (((1,), (1,)), ((), ()))
    bufs = (lat_buf, kr_buf, fk_buf, fv_buf, lf_buf)

    def start_chunk(step, sl):
        step = step % total
        bb = step // nch
        first_page = (nch - 1 - step % nch) * cp
        for p in range(cp):
            page = pt_ref[bb, first_page + p]
            dsts = (lat_buf.at[sl, p], kr_buf.at[sl, :, pl.ds(p * PAGE_SIZE, PAGE_SIZE)],
                    fk_buf.at[sl, p], fv_buf.at[sl, p], lf_buf.at[sl, p])
            for a, (hbm, dst) in enumerate(zip((lat_hbm, kr_hbm, fk_hbm, fv_hbm, lf_hbm), dsts)):
                pltpu.make_async_copy(hbm.at[page], dst, sems.at[a, sl]).start()

    def wait_chunk(sl):
        for a, buf in enumerate(bufs):
            pltpu.make_async_copy(buf.at[sl], buf.at[sl], sems.at[a, sl]).wait()

    @pl.when(t == 0)
    def _():
        for ahead in range(DECODE_SLOTS - 1):
            start_chunk(ahead, ahead)

    qlat = qlat_ref[...]
    qrope = qrope_ref[:, :ROPE_DIM]
    fq = fq_ref[...]
    head_col = lax.broadcasted_iota(jnp.int32, (FOX_HEADS, 1), 0)
    kv0 = head_col < FOX_GROUP

    @pl.when(j == 0)
    def _():
        ckvn = ckvn_ref[...].astype(BF16).astype(F32)
        krn = krn_ref[:, :ROPE_DIM].astype(BF16).astype(F32)
        m_m[...] = (jnp.sum(qlat.astype(F32) * ckvn, axis=-1, keepdims=True)
                    + jnp.sum(qrope.astype(F32) * krn, axis=-1, keepdims=True))
        l_m[...] = jnp.ones_like(l_m)
        acc_m[...] = jnp.broadcast_to(ckvn, acc_m.shape)
        fkn = fkn_ref[...].astype(BF16).astype(F32)
        fvn = fvn_ref[...].astype(BF16).astype(F32)
        fqf = fq.astype(F32)
        m_f[...] = jnp.where(kv0, jnp.sum(fqf * fkn[:, :HEAD_DIM], axis=-1, keepdims=True),
                             jnp.sum(fqf * fkn[:, HEAD_DIM:], axis=-1, keepdims=True))
        l_f[...] = jnp.ones_like(l_f)
        acc_f[...] = jnp.where(kv0, fvn[:, :HEAD_DIM], fvn[:, HEAD_DIM:])
        carry[...] = lfn_ref[...]

    wait_chunk(slot)
    start_chunk(t + DECODE_SLOTS - 1, (t + DECODE_SLOTS - 1) % DECODE_SLOTS)

    latb = lat_buf[slot].reshape(kc, KV_LORA).astype(BF16)
    krb = kr_buf[slot].astype(BF16)
    s = (lax.dot_general(qlat, latb, nt, preferred_element_type=F32)
         + jnp.dot(qrope, krb, preferred_element_type=F32))
    m_old = m_m[...]
    m_new = jnp.maximum(m_old, jnp.max(s, axis=-1, keepdims=True))
    alpha = jnp.exp(m_old - m_new)
    p = jnp.exp(s - m_new)
    l_m[...] = alpha * l_m[...] + jnp.sum(p, axis=-1, keepdims=True)
    acc_m[...] = alpha * acc_m[...] + jnp.dot(p.astype(BF16), latb, preferred_element_type=F32)
    m_m[...] = m_new

    lf = lf_buf[slot]
    x1, x2, x3 = _split3(lf.reshape(cp * FOX_HEADS, PAGE_SIZE))
    jr = lax.broadcasted_iota(jnp.int32, (PAGE_SIZE, PAGE_SIZE), 0)
    kcol = lax.broadcasted_iota(jnp.int32, (PAGE_SIZE, PAGE_SIZE), 1)
    later = (jr > kcol).astype(BF16)
    within = (jnp.dot(x1, later, preferred_element_type=F32)
              + jnp.dot(x2, later, preferred_element_type=F32)
              + jnp.dot(x3, later, preferred_element_type=F32))
    off = carry[...]
    tiles = [None] * cp
    for pg in range(cp - 1, -1, -1):
        tiles[pg] = within[pg * FOX_HEADS:(pg + 1) * FOX_HEADS, :] + off
        off = off + jnp.sum(lf[pg], axis=-1, keepdims=True)
    carry[...] = off
    bias = jnp.concatenate(tiles, axis=1)

    def kv_head(buf, n):
        rows = buf[slot, :, pl.ds(n, PAGE_SIZE, stride=FOX_KV_HEADS), :]
        return rows.reshape(kc, HEAD_DIM).astype(BF16)

    s = jnp.where(kv0, lax.dot_general(fq, kv_head(fk_buf, 0), nt, preferred_element_type=F32),
                  lax.dot_general(fq, kv_head(fk_buf, 1), nt, preferred_element_type=F32)) + bias
    m_old = m_f[...]
    m_new = jnp.maximum(m_old, jnp.max(s, axis=-1, keepdims=True))
    alpha = jnp.exp(m_old - m_new)
    p = jnp.exp(s - m_new)
    pb = p.astype(BF16)
    l_f[...] = alpha * l_f[...] + jnp.sum(p, axis=-1, keepdims=True)
    pv = jnp.where(kv0, jnp.dot(pb, kv_head(fv_buf, 0), preferred_element_type=F32),
                   jnp.dot(pb, kv_head(fv_buf, 1), preferred_element_type=F32))
    acc_f[...] = alpha * acc_f[...] + pv
    m_f[...] = m_new

    @pl.when(j == nch - 1)
    def _():
        olat_ref[...] = acc_m[...] / l_m[...]
        ofox_ref[...] = acc_f[...] / l_f[...]

    @pl.when(t == total - 1)
    def _():
        for ahead in range(1, DECODE_SLOTS):
            wait_chunk((t + ahead) % DECODE_SLOTS)


def _decode_attention(page_table, qlat, qrope, fq, ckv_new, kr_new, fk_new, fv_new, lf_new,
                      lat_c, kr_c, fk_c, fv_c, lf_c):
    nb, n_pages = page_table.shape
    cp = DECODE_PAGES
    ns = DECODE_SLOTS
    kvw = FOX_KV_HEADS * HEAD_DIM
    kv_rows = FOX_KV_HEADS * PAGE_SIZE
    per_b = lambda *shape: pl.BlockSpec((None,) + shape, lambda b, j, pt: (b, 0, 0))
    hbm = pl.BlockSpec(memory_space=pl.ANY)
    grid_spec = pltpu.PrefetchScalarGridSpec(
        num_scalar_prefetch=1,
        grid=(nb, n_pages // cp),
        in_specs=[per_b(MLA_HEADS, KV_LORA), per_b(MLA_HEADS, LANES), per_b(FOX_HEADS, HEAD_DIM),
                  per_b(1, KV_LORA), per_b(1, LANES), per_b(1, kvw), per_b(1, kvw),
                  per_b(FOX_HEADS, 1), hbm, hbm, hbm, hbm, hbm],
        out_specs=[per_b(MLA_HEADS, KV_LORA), per_b(FOX_HEADS, HEAD_DIM)],
        scratch_shapes=[
            pltpu.VMEM((ns, cp, PAGE_SIZE, KV_LORA), F32),
            pltpu.VMEM((ns, ROPE_DIM, cp * PAGE_SIZE), F32),
            pltpu.VMEM((ns, cp, kv_rows, HEAD_DIM), F32),
            pltpu.VMEM((ns, cp, kv_rows, HEAD_DIM), F32),
            pltpu.VMEM((ns, cp, FOX_HEADS, PAGE_SIZE), F32),
            pltpu.SemaphoreType.DMA((5, ns)),
            pltpu.VMEM((MLA_HEADS, 1), F32), pltpu.VMEM((MLA_HEADS, 1), F32),
            pltpu.VMEM((MLA_HEADS, KV_LORA), F32),
            pltpu.VMEM((FOX_HEADS, 1), F32), pltpu.VMEM((FOX_HEADS, 1), F32),
            pltpu.VMEM((FOX_HEADS, HEAD_DIM), F32),
            pltpu.VMEM((FOX_HEADS, 1), F32),
        ],
    )
    return pl.pallas_call(
        functools.partial(_decode_kernel, cp=cp, nb=nb, nch=n_pages // cp),
        grid_spec=grid_spec,
        out_shape=[jax.ShapeDtypeStruct((nb, MLA_HEADS, KV_LORA), F32),
                   jax.ShapeDtypeStruct((nb, FOX_HEADS, HEAD_DIM), F32)],
        compiler_params=_params("arbitrary", "arbitrary"),
        name="decode_attention",
    )(page_table, qlat, qrope, fq, ckv_new, kr_new, fk_new, fv_new, lf_new,
      lat_c, kr_c, fk_c, fv_c, lf_c)


def _swap_halves(x):
    half = x.shape[-1] // 2
    return jnp.concatenate([x[..., half:], x[..., :half]], axis=-1)


def _relayout_w_in(w):
    o = 0
    segs = {}
    for name, width in (("cq", Q_LORA), ("ckv", KV_LORA), ("kr", ROPE_DIM),
                        ("fq", FOX_HEADS * HEAD_DIM), ("fk", FOX_KV_HEADS * HEAD_DIM),
                        ("fv", FOX_KV_HEADS * HEAD_DIM), ("ff", FOX_HEADS)):
        segs[name] = w[:, o:o + width]
        o += width
    zeros = lambda n: jnp.zeros((w.shape[0], n), w.dtype)
    cols = [segs["cq"], segs["ckv"], segs["fq"], segs["fk"], segs["fv"],
            segs["kr"], _swap_halves(segs["kr"]), segs["ff"], zeros(LANES - FOX_HEADS),
            zeros(Z_WIDTH - Z_FF - LANES)]
    return jnp.concatenate(cols, axis=1).astype(BF16)


def _relayout_w_q_up(w):
    nope, rope = w[..., :HEAD_DIM], w[..., HEAD_DIM:]
    ext = jnp.concatenate([nope, rope, _swap_halves(rope)], axis=-1)
    return ext.reshape(w.shape[0], MLA_HEADS * QH).astype(BF16)


def _build_w_kv(w_uk, w_uv):
    wk = jnp.zeros((KVA_WIDTH, MLA_HEADS, QH), F32)
    wk = wk.at[:KV_LORA, :, :HEAD_DIM].set(w_uk)
    eye = jnp.broadcast_to(jnp.eye(ROPE_DIM, dtype=F32)[:, None, :], (ROPE_DIM, MLA_HEADS, ROPE_DIM))
    wk = wk.at[KV_LORA:KV_LORA + ROPE_DIM, :, HEAD_DIM:HEAD_DIM + ROPE_DIM].set(eye)
    wv = jnp.zeros((KVA_WIDTH, MLA_HEADS, HEAD_DIM), F32).at[:KV_LORA].set(w_uv)
    return jnp.concatenate([wk.reshape(KVA_WIDTH, -1), wv.reshape(KVA_WIDTH, -1)], axis=1).astype(BF16)


def _rope_tables(pos):
    half = ROPE_DIM // 2
    inv_freq = jnp.power(ROPE_THETA, -jnp.arange(half, dtype=F32) * 2.0 / ROPE_DIM)
    ang = pos.astype(F32)[:, None] * inv_freq[None, :]
    cos, sin = jnp.cos(ang), jnp.sin(ang)
    z = jnp.zeros((pos.shape[0], LANES - ROPE_DIM), F32)
    return (jnp.concatenate([cos, cos, z], axis=1), jnp.concatenate([-sin, sin, z], axis=1))


def _layer(x, mods, rows_per_batch, pos_tables, pos_blocks, tm, w, attend):
    sh1, sc1, ga1, sh2, sc2, ga2 = [_Mod(m, rows_per_batch) for m in mods]
    cs1, cs2 = pos_tables
    tm_small = min(tm, 256)
    h = _norm_mod(x, w["g_attn"], sc1, sh1, tm_small)
    z = _dense(h, w["w_in"], tm=tm, tn=512, out_dtype=F32, name="in_proj")
    (cqn, lat, kva, kr, fq, fk, fv, fkb, fvb, lf, lf128) = _post_in(
        z, w["g_q"], w["g_kv"], w["b_f"], cs1, cs2, tm_small, pos_blocks * (tm // tm_small) if pos_blocks > 1 else 1)
    pos_spec = pl.BlockSpec((tm, LANES), lambda i, j: (i % pos_blocks, 0))
    q256 = _dense(cqn, w["w_q"], tm=tm, tn=512, out_dtype=BF16, name="q_up",
                  epi=_epi_q_rope, extras=((cs1, pos_spec), (cs2, pos_spec)))
    ym, yf = attend(q256, kva, fq, fkb, fvb, lf128, lat, kr, fk, fv, lf)
    y = _norm2(ym, yf, w["g_mla"], w["g_fox"], tm_small)
    x = _dense(y, w["w_out"], tm=tm, tn=512, out_dtype=F32, name="out_proj", epi=_epi_residual,
               extras=((x, pl.BlockSpec((tm, 512), lambda i, j: (i, j))), (ga1.arr, ga1.spec(tm, 512))))
    h = _norm_mod(x, w["g_ffn"], sc2, sh2, tm_small)
    a = _gate_up(h, w["w_gate"], w["w_up"], tm, 256)
    tmd = min(tm, 512)
    x = _dense(a, w["w_down"], tm=tmd, tn=256, out_dtype=F32, name="ffn_down", epi=_epi_residual,
               extras=((x, pl.BlockSpec((tmd, 256), lambda i, j: (i, j))), (ga2.arr, ga2.spec(tmd, 256))))
    return x, (lat, kr, fk, fv, lf)


def kernel(x_prompt, x_sample, c_prompt, c_sample, cache_mla_latent, cache_mla_krope, cache_fox_k,
           cache_fox_v, cache_fox_logf, page_table, w_ada, b_ada, g_attn_norm, w_in, g_q_norm, w_q_up,
           g_kv_norm, w_uk, w_uv, b_forget, g_mla_out, g_fox_out, w_out, g_ffn_norm, w_gate, w_up,
           w_down, g_final):
    nbp, seq, d = x_prompt.shape
    nbs = x_sample.shape[0]
    depth = w_ada.shape[0]
    n_phys = cache_mla_latent.shape[1]
    past = page_table.shape[1] * PAGE_SIZE
    kvw = FOX_KV_HEADS * HEAD_DIM

    xp = x_prompt.reshape(nbp * seq, d)
    xs = x_sample.reshape(nbs, d)
    c_all = jnp.concatenate([c_prompt, c_sample], axis=0)
    tables_p = _rope_tables(jnp.arange(seq, dtype=jnp.int32))
    tables_s = tuple(jnp.broadcast_to(t, (nbs, LANES))
                     for t in _rope_tables(past + jnp.arange(1, dtype=jnp.int32)))
    tm_p = 1024
    st_p, st_s = [], []
    for l in range(depth):
        w = dict(g_attn=g_attn_norm[l], w_in=_relayout_w_in(w_in[l]), g_q=g_q_norm[l],
                 w_q=_relayout_w_q_up(w_q_up[l]), g_kv=g_kv_norm[l], b_f=b_forget[l],
                 g_mla=g_mla_out[l], g_fox=g_fox_out[l], w_out=w_out[l], g_ffn=g_ffn_norm[l],
                 w_gate=w_gate[l], w_up=w_up[l], w_down=w_down[l].astype(BF16))
        w_kv = _build_w_kv(w_uk[l], w_uv[l])
        w_uk2d = w_uk[l].reshape(KV_LORA, MLA_HEADS * HEAD_DIM)
        w_uv2d = w_uv[l].reshape(KV_LORA, MLA_HEADS * HEAD_DIM)
        mod = _ada(c_all, w_ada[l], b_ada[l])
        mods_p = [mod[:nbp, i * d:(i + 1) * d] for i in range(N_MOD)]
        mods_s = [mod[nbp:, i * d:(i + 1) * d] for i in range(N_MOD)]

        def attend_prompt(q256, kva, fq, fkb, fvb, lf128, *_):
            kv = _dense(kva, w_kv, tm=tm_p, tn=512, out_dtype=BF16, name="kv_up")
            ym = _flash_mla(q256.reshape(nbp, seq, -1), kv.reshape(nbp, seq, -1), FLASH_BLOCK)
            e, et = _suffix(lf128.reshape(nbp, seq, LANES))
            yf = _flash_fox(fq.reshape(nbp, seq, -1), fkb.reshape(nbp, seq, kvw),
                            fvb.reshape(nbp, seq, kvw), e, et, FLASH_BLOCK)
            return ym.reshape(nbp * seq, -1), yf.reshape(nbp * seq, -1)

        def attend_sample(q256, kva, fq, fkb, fvb, lf128, lat, kr, fk, fv, lf):
            qlat = _absorb(q256, w_uk2d).reshape(nbs, MLA_HEADS, KV_LORA)
            qrope = q256.reshape(nbs, MLA_HEADS, QH)[:, :, LANES:]
            o_lat, o_fox = _decode_attention(
                page_table, qlat, qrope, fq.reshape(nbs, FOX_HEADS, HEAD_DIM),
                lat.reshape(nbs, 1, KV_LORA), kva[:, KV_LORA:].astype(F32).reshape(nbs, 1, LANES),
                fk.reshape(nbs, 1, kvw), fv.reshape(nbs, 1, kvw), lf.reshape(nbs, FOX_HEADS, 1),
                cache_mla_latent[l], jnp.swapaxes(cache_mla_krope[l], 1, 2),
                cache_fox_k[l].reshape(n_phys, PAGE_SIZE * FOX_KV_HEADS, HEAD_DIM),
                cache_fox_v[l].reshape(n_phys, PAGE_SIZE * FOX_KV_HEADS, HEAD_DIM),
                jnp.swapaxes(cache_fox_logf[l], 1, 2))
            ym = _value_up(o_lat.reshape(nbs, MLA_HEADS * KV_LORA), w_uv2d)
            return ym, o_fox.reshape(nbs, FOX_HEADS * HEAD_DIM)

        xp, sp = _layer(xp, mods_p, seq, tables_p, seq // tm_p, tm_p, w, attend_prompt)
        xs, ss = _layer(xs, mods_s, 1, tables_s, 1, nbs, w, attend_sample)
        st_p.append(sp)
        st_s.append(ss)

    y_prompt = _norm(xp, g_final, 256).reshape(nbp, seq, d)
    y_sample = _norm(xs, g_final, nbs).reshape(nbs, 1, d)

    def stack(states, idx, shape):
        return jnp.stack([s[idx].reshape(shape) for s in states])

    outs = [y_prompt, y_sample]
    for states, lead in ((st_p, (nbp, seq)), (st_s, (nbs, 1))):
        outs += [stack(states, 0, lead + (KV_LORA,)), stack(states, 1, lead + (ROPE_DIM,)),
                 stack(states, 2, lead + (FOX_KV_HEADS, HEAD_DIM)),
                 stack(states, 3, lead + (FOX_KV_HEADS, HEAD_DIM)),
                 stack(states, 4, lead + (FOX_HEADS,))]
    return tuple(outs)
```

```python
import functools
import math

import jax
import jax.numpy as jnp
from jax import lax
from jax.experimental import pallas as pl
from jax.experimental.pallas import tpu as pltpu

F32 = jnp.float32
BF16 = jnp.bfloat16

D_MODEL = 4096
HEAD_DIM = 128
MLA_HEADS = 16
FOX_HEADS = 16
FOX_KV_HEADS = 2
FOX_GROUP = FOX_HEADS // FOX_KV_HEADS
Q_LORA = 1536
KV_LORA = 512
ROPE_DIM = 64
ROPE_THETA = 10000.0
MLA_SCALE = (HEAD_DIM + ROPE_DIM) ** -0.5
FOX_SCALE = HEAD_DIM ** -0.5
N_MOD = 6
PAGE_SIZE = 128
EPS = 1e-6

LANES = 128
VMEM_LIMIT = 56 * 1024 * 1024

Z_CQ = 0
Z_CKV = Z_CQ + Q_LORA
Z_FQ = Z_CKV + KV_LORA
Z_FK = Z_FQ + FOX_HEADS * HEAD_DIM
Z_FV = Z_FK + FOX_KV_HEADS * HEAD_DIM
Z_KR = Z_FV + FOX_KV_HEADS * HEAD_DIM
Z_FF = Z_KR + LANES
Z_WIDTH = 5120
KVA_WIDTH = KV_LORA + LANES
QH = 2 * LANES
FLASH_BLOCK = 1024
GATE_UP_ROW_TILES = 8
DECODE_PAGES = 16
DECODE_SLOTS = 3


def _params(*sem):
    return pltpu.CompilerParams(dimension_semantics=sem, vmem_limit_bytes=VMEM_LIMIT)


def _rms(x):
    return x * lax.rsqrt(jnp.mean(x * x, axis=-1, keepdims=True) + EPS)


def _rope_lanes(g, cs1, cs2):
    return g * cs1 + pltpu.roll(g, ROPE_DIM, axis=1) * cs2


def _ada_kernel(c_ref, w_ref, b_ref, o_ref):
    c = c_ref[...]
    a = (c * jax.nn.sigmoid(c)).astype(BF16)
    o_ref[...] = jnp.dot(a, w_ref[...].astype(BF16), preferred_element_type=F32) + b_ref[...]


def _ada(c, w, b):
    m, k = c.shape
    n = w.shape[1]
    tn = 512
    return pl.pallas_call(
        _ada_kernel,
        grid=(n // tn,),
        in_specs=[pl.BlockSpec((m, k), lambda j: (0, 0)),
                  pl.BlockSpec((k, tn), lambda j: (0, j)),
                  pl.BlockSpec((1, tn), lambda j: (0, j))],
        out_specs=pl.BlockSpec((m, tn), lambda j: (0, j)),
        out_shape=jax.ShapeDtypeStruct((m, n), F32),
        compiler_params=_params("arbitrary"),
        name="ada",
    )(c, w, b.reshape(1, n))


class _Mod:
    def __init__(self, arr, rows_per_batch):
        self.rows_per_batch = rows_per_batch
        if rows_per_batch == 1:
            self.arr = arr
        else:
            self.arr = arr.reshape(arr.shape[0], 1, arr.shape[1])

    def spec(self, tm, tn, row_only=False, clamp=lambda i: i):
        if self.rows_per_batch == 1:
            if row_only:
                return pl.BlockSpec((tm, tn), lambda i: (clamp(i), 0))
            return pl.BlockSpec((tm, tn), lambda i, j: (i, j))
        per = self.rows_per_batch // tm
        if row_only:
            return pl.BlockSpec((None, 1, tn), lambda i: (clamp(i) // per, 0, 0))
        return pl.BlockSpec((None, 1, tn), lambda i, j: (i // per, 0, j))


def _norm_mod_kernel(x_ref, g_ref, sc_ref, sh_ref, o_ref):
    y = _rms(x_ref[...]) * g_ref[...]
    o_ref[...] = (y * (1.0 + sc_ref[...]) + sh_ref[...]).astype(o_ref.dtype)


def _norm_kernel(x_ref, g_ref, o_ref):
    o_ref[...] = (_rms(x_ref[...]) * g_ref[...]).astype(o_ref.dtype)


def _norm_mod_pad_kernel(x_ref, g_ref, sc_ref, sh_ref, o_ref, *, n_real):
    @pl.when(pl.program_id(0) < n_real)
    def _():
        _norm_mod_kernel(x_ref, g_ref, sc_ref, sh_ref, o_ref)

    @pl.when(pl.program_id(0) >= n_real)
    def _():
        o_ref[...] = jnp.zeros_like(o_ref)


def _norm_mod(x, g, sc, sh, tm, extra_rows=0):
    t, d = x.shape
    assert t % tm == 0 and extra_rows % tm == 0
    n_real = t // tm
    clamp = lambda i: jnp.minimum(i, n_real - 1)
    body = _norm_mod_kernel if not extra_rows else functools.partial(_norm_mod_pad_kernel, n_real=n_real)
    return pl.pallas_call(
        body,
        grid=(n_real + extra_rows // tm,),
        in_specs=[pl.BlockSpec((tm, d), lambda i: (clamp(i), 0)),
                  pl.BlockSpec((1, d), lambda i: (0, 0)),
                  sc.spec(tm, d, row_only=True, clamp=clamp),
                  sh.spec(tm, d, row_only=True, clamp=clamp)],
        out_specs=pl.BlockSpec((tm, d), lambda i: (i, 0)),
        out_shape=jax.ShapeDtypeStruct((t + extra_rows, d), BF16),
        compiler_params=_params("arbitrary"),
        name="norm_mod",
    )(x, g.reshape(1, d), sc.arr, sh.arr)


def _place_rows_kernel(dst_ref, src_ref, o_ref):
    del dst_ref
    o_ref[...] = src_ref[...]


def _place_rows(dst, src):
    n, d = src.shape
    assert dst.shape[0] % n == 0
    last = dst.shape[0] // n - 1
    return pl.pallas_call(
        _place_rows_kernel,
        grid=(1,),
        in_specs=[pl.BlockSpec(memory_space=pl.ANY), pl.BlockSpec((n, d), lambda i: (0, 0))],
        out_specs=pl.BlockSpec((n, d), lambda i: (last, 0)),
        out_shape=jax.ShapeDtypeStruct(dst.shape, dst.dtype),
        input_output_aliases={0: 0},
        compiler_params=_params("arbitrary"),
        name="place_rows",
    )(dst, src)


def _norm(x, g, tm):
    t, d = x.shape
    return pl.pallas_call(
        _norm_kernel,
        grid=(t // tm,),
        in_specs=[pl.BlockSpec((tm, d), lambda i: (i, 0)),
                  pl.BlockSpec((1, d), lambda i: (0, 0))],
        out_specs=pl.BlockSpec((tm, d), lambda i: (i, 0)),
        out_shape=jax.ShapeDtypeStruct((t, d), F32),
        compiler_params=_params("arbitrary"),
        name="final_norm",
    )(x, g.reshape(1, d))


def _norm2_kernel(ym_ref, yf_ref, gm_ref, gf_ref, o_ref):
    w = ym_ref.shape[1]
    o_ref[:, :w] = (_rms(ym_ref[...]) * gm_ref[...]).astype(o_ref.dtype)
    o_ref[:, w:] = (_rms(yf_ref[...]) * gf_ref[...]).astype(o_ref.dtype)


def _norm2(ym, yf, gm, gf, tm):
    t, w = ym.shape
    return pl.pallas_call(
        _norm2_kernel,
        grid=(t // tm,),
        in_specs=[pl.BlockSpec((tm, w), lambda i: (i, 0)),
                  pl.BlockSpec((tm, w), lambda i: (i, 0)),
                  pl.BlockSpec((1, w), lambda i: (0, 0)),
                  pl.BlockSpec((1, w), lambda i: (0, 0))],
        out_specs=pl.BlockSpec((tm, 2 * w), lambda i: (i, 0)),
        out_shape=jax.ShapeDtypeStruct((t, 2 * w), BF16),
        compiler_params=_params("arbitrary"),
        name="mix_norm",
    )(ym, yf, gm.reshape(1, w), gf.reshape(1, w))


def _mm_kernel(a_ref, w_ref, *rest, epi, n_extra):
    o_ref = rest[n_extra]
    acc = jnp.dot(a_ref[...].astype(BF16), w_ref[...].astype(BF16),
                  preferred_element_type=F32)
    if epi is not None:
        acc = epi(acc, *rest[:n_extra])
    o_ref[...] = acc.astype(o_ref.dtype)


def _dense(a, w, *, tm, tn, out_dtype, name, epi=None, extras=(), rows=None, first_row=0):
    k = a.shape[1]
    m = a.shape[0] if rows is None else rows
    n = w.shape[1]
    assert m % tm == 0 and n % tn == 0 and first_row % tm == 0
    blk0 = first_row // tm
    return pl.pallas_call(
        functools.partial(_mm_kernel, epi=epi, n_extra=len(extras)),
        grid=(m // tm, n // tn),
        in_specs=[pl.BlockSpec((tm, k), lambda i, j: (i + blk0, 0)),
                  pl.BlockSpec((k, tn), lambda i, j: (0, j))] + [s for _, s in extras],
        out_specs=pl.BlockSpec((tm, tn), lambda i, j: (i, j)),
        out_shape=jax.ShapeDtypeStruct((m, n), out_dtype),
        compiler_params=_params("arbitrary", "arbitrary"),
        name=name,
    )(a, w, *[x for x, _ in extras])


def _epi_residual(acc, x_ref, ga_ref):
    return x_ref[...] + ga_ref[...] * acc


def _epi_q_rope(acc, cs1_ref, cs2_ref):
    cs1 = cs1_ref[...]
    cs2 = cs2_ref[...]
    parts = []
    for h in range(acc.shape[1] // QH):
        parts.append(acc[:, h * QH:h * QH + LANES])
        parts.append(_rope_lanes(acc[:, h * QH + LANES:(h + 1) * QH], cs1, cs2))
    return jnp.concatenate(parts, axis=1) * MLA_SCALE


def _gate_up_kernel(a_ref, wg_ref, wu_ref, o_ref):
    a = a_ref[...]
    g = jnp.dot(a, wg_ref[...].astype(BF16), preferred_element_type=F32)
    u = jnp.dot(a, wu_ref[...].astype(BF16), preferred_element_type=F32)
    o_ref[...] = (g * jax.nn.sigmoid(g) * u).astype(o_ref.dtype)


def _gate_up(a, wg, wu, tm, tn):
    m, k = a.shape
    n = wg.shape[1]
    assert m % tm == 0 and n % tn == 0
    return pl.pallas_call(
        _gate_up_kernel,
        grid=(m // tm, n // tn),
        in_specs=[pl.BlockSpec((tm, k), lambda i, j: (i, 0)),
                  pl.BlockSpec((k, tn), lambda i, j: (0, j)),
                  pl.BlockSpec((k, tn), lambda i, j: (0, j))],
        out_specs=pl.BlockSpec((tm, tn), lambda i, j: (i, j)),
        out_shape=jax.ShapeDtypeStruct((m, n), BF16),
        compiler_params=_params("arbitrary", "arbitrary"),
        name="ffn_gate_up",
    )(a, wg, wu)


def _post_in_kernel(z_ref, gq_ref, gkv_ref, bf_ref, cs1_ref, cs2_ref,
                    cqn_ref, lat_ref, kva_ref, kr_ref, fq_ref, fk_ref, fv_ref,
                    fkb_ref, fvb_ref, lf_ref, lf128_ref):
    cqn_ref[...] = (_rms(z_ref[:, Z_CQ:Z_CKV]) * gq_ref[...]).astype(BF16)
    lat = _rms(z_ref[:, Z_CKV:Z_FQ]) * gkv_ref[...]
    lat_ref[...] = lat
    kr = _rope_lanes(z_ref[:, Z_KR:Z_FF], cs1_ref[...], cs2_ref[...])
    kr_ref[...] = kr[:, :ROPE_DIM]
    kva_ref[:, :KV_LORA] = lat.astype(BF16)
    kva_ref[:, KV_LORA:] = kr.astype(BF16)
    fq_ref[...] = (z_ref[:, Z_FQ:Z_FK] * FOX_SCALE).astype(BF16)
    fk = z_ref[:, Z_FK:Z_FV]
    fv = z_ref[:, Z_FV:Z_KR]
    fk_ref[...] = fk
    fv_ref[...] = fv
    fkb_ref[...] = fk.astype(BF16)
    fvb_ref[...] = fv.astype(BF16)
    x = z_ref[:, Z_FF:Z_FF + LANES] + bf_ref[...]
    lf = jnp.minimum(x, 0.0) - jnp.log1p(jnp.exp(-jnp.abs(x)))
    lane = lax.broadcasted_iota(jnp.int32, lf.shape, 1)
    lf = jnp.where(lane < FOX_HEADS, lf, 0.0)
    lf128_ref[...] = lf
    lf_ref[...] = lf[:, :FOX_HEADS]


def _post_in(z, g_q, g_kv, b_f, cs1, cs2, tm, pos_blocks):
    t = z.shape[0]
    kvw = FOX_KV_HEADS * HEAD_DIM
    row = lambda w: pl.BlockSpec((tm, w), lambda i: (i, 0))
    const = lambda w: pl.BlockSpec((1, w), lambda i: (0, 0))
    pos = pl.BlockSpec((tm, LANES), lambda i: (i % pos_blocks, 0))
    bf = jnp.zeros((1, LANES), F32).at[0, :FOX_HEADS].set(b_f)
    shapes = [((t, Q_LORA), BF16), ((t, KV_LORA), F32), ((t, KVA_WIDTH), BF16),
              ((t, ROPE_DIM), F32), ((t, FOX_HEADS * HEAD_DIM), BF16),
              ((t, kvw), F32), ((t, kvw), F32), ((t, kvw), BF16), ((t, kvw), BF16),
              ((t, FOX_HEADS), F32), ((t, LANES), F32)]
    return pl.pallas_call(
        _post_in_kernel,
        grid=(t // tm,),
        in_specs=[row(Z_WIDTH), const(Q_LORA), const(KV_LORA), const(LANES), pos, pos],
        out_specs=[row(s[1]) for s, _ in shapes],
        out_shape=[jax.ShapeDtypeStruct(s, dt) for s, dt in shapes],
        compiler_params=_params("arbitrary"),
        name="post_in",
    )(z, g_q.reshape(1, Q_LORA), g_kv.reshape(1, KV_LORA), bf, cs1, cs2)


def _split3(x):
    a1 = x.astype(BF16)
    r1 = x - a1.astype(F32)
    a2 = r1.astype(BF16)
    a3 = (r1 - a2.astype(F32)).astype(BF16)
    return a1, a2, a3


def _suffix_kernel(lf_ref, e_ref, et_ref, *, rows):
    s = lf_ref.shape[0]
    parts = _split3(lf_ref[...])
    blocks = []
    for r0 in range(0, s, rows):
        row = lax.broadcasted_iota(jnp.int32, (rows, s), 0) + r0
        col = lax.broadcasted_iota(jnp.int32, (rows, s), 1)
        upper = (col > row).astype(BF16)
        blocks.append(sum(jnp.dot(upper, a, preferred_element_type=F32) for a in parts))
    e = jnp.concatenate(blocks, axis=0)
    e_ref[...] = e[:, :FOX_HEADS]
    et_ref[...] = e.T[:FOX_HEADS, :]


def _suffix(lf128):
    b, s, _ = lf128.shape
    return pl.pallas_call(
        functools.partial(_suffix_kernel, rows=512),
        grid=(b,),
        in_specs=[pl.BlockSpec((None, s, LANES), lambda i: (i, 0, 0))],
        out_specs=[pl.BlockSpec((None, s, FOX_HEADS), lambda i: (i, 0, 0)),
                   pl.BlockSpec((None, FOX_HEADS, s), lambda i: (i, 0, 0))],
        out_shape=[jax.ShapeDtypeStruct((b, s, FOX_HEADS), F32),
                   jax.ShapeDtypeStruct((b, FOX_HEADS, s), F32)],
        compiler_params=_params("arbitrary"),
        name="suffix_logf",
    )(lf128)


def _flash(q_ref, k_ref, v_ref, o_ref, ek_ref, eq, *, blk):
    s_len = q_ref.shape[0]
    dv = v_ref.shape[1]
    row = lax.broadcasted_iota(jnp.int32, (blk, blk), 0)
    col = lax.broadcasted_iota(jnp.int32, (blk, blk), 1)
    causal = row >= col
    for qi in range(s_len // blk):
        q = q_ref[qi * blk:(qi + 1) * blk, :]
        eq_blk = None if eq is None else eq[qi * blk:(qi + 1) * blk, :]

        def step(ki, carry, diagonal):
            m, l, acc = carry
            k0 = ki * blk if diagonal else pl.multiple_of(ki * blk, blk)
            k = k_ref[pl.ds(k0, blk), :]
            v = v_ref[pl.ds(k0, blk), :]
            s = lax.dot_general(q, k, (((1,), (1,)), ((), ())), preferred_element_type=F32)
            if ek_ref is not None:
                s = s + ek_ref[pl.ds(ki, 1), :] - eq_blk
            if diagonal:
                s = jnp.where(causal, s, -jnp.inf)
            m_new = jnp.maximum(m, jnp.max(s, axis=-1, keepdims=True))
            alpha = jnp.exp(m - m_new)
            p = jnp.exp(s - m_new)
            l = alpha * l + jnp.sum(p, axis=-1, keepdims=True)
            acc = alpha * acc + jnp.dot(p.astype(BF16), v, preferred_element_type=F32)
            return m_new, l, acc

        carry = (jnp.full((blk, 1), -jnp.inf, F32), jnp.zeros((blk, 1), F32),
                 jnp.zeros((blk, dv), F32))
        if qi > 0:
            carry = lax.fori_loop(0, qi, lambda ki, c: step(ki, c, False), carry)
        _, l, acc = step(qi, carry, True)
        o_ref[qi * blk:(qi + 1) * blk, :] = acc / l


def _flash_mla_kernel(q_ref, k_ref, v_ref, o_ref, *, blk):
    _flash(q_ref, k_ref, v_ref, o_ref, None, None, blk=blk)


def _flash_fox_kernel(q_ref, k_ref, v_ref, ek_ref, e_ref, o_ref, *, blk):
    h = pl.program_id(1)
    e = e_ref[...]
    lane = lax.broadcasted_iota(jnp.int32, e.shape, 1)
    eq = jnp.sum(jnp.where(lane == h, e, 0.0), axis=-1, keepdims=True)
    _flash(q_ref, k_ref, v_ref, o_ref, ek_ref, eq, blk=blk)


def _kv_up_kernel(a_ref, wk_ref, wv_ref, k_ref, v_ref):
    c = a_ref[:, :KV_LORA]
    kr = a_ref[:, KV_LORA:]
    kn = jnp.dot(c, wk_ref[...].astype(BF16), preferred_element_type=F32).astype(BF16)
    parts = []
    for h in range(kn.shape[1] // HEAD_DIM):
        parts += [kn[:, h * HEAD_DIM:(h + 1) * HEAD_DIM], kr]
    k_ref[...] = jnp.concatenate(parts, axis=1)
    v_ref[...] = jnp.dot(c, wv_ref[...].astype(BF16), preferred_element_type=F32).astype(BF16)


def _kv_up(kva, w_uk2d, w_uv2d, tm):
    t = kva.shape[0]
    heads = 2
    tn = heads * HEAD_DIM
    return pl.pallas_call(
        _kv_up_kernel,
        grid=(t // tm, MLA_HEADS // heads),
        in_specs=[pl.BlockSpec((tm, KVA_WIDTH), lambda i, j: (i, 0)),
                  pl.BlockSpec((KV_LORA, tn), lambda i, j: (0, j)),
                  pl.BlockSpec((KV_LORA, tn), lambda i, j: (0, j))],
        out_specs=[pl.BlockSpec((tm, heads * QH), lambda i, j: (i, j)),
                   pl.BlockSpec((tm, tn), lambda i, j: (i, j))],
        out_shape=[jax.ShapeDtypeStruct((t, MLA_HEADS * QH), BF16),
                   jax.ShapeDtypeStruct((t, MLA_HEADS * HEAD_DIM), BF16)],
        compiler_params=_params("arbitrary", "arbitrary"),
        name="kv_up",
    )(kva, w_uk2d, w_uv2d)


def _flash_mla(q, k, v, blk):
    b, s, _ = q.shape
    return pl.pallas_call(
        functools.partial(_flash_mla_kernel, blk=blk),
        grid=(b, MLA_HEADS),
        in_specs=[pl.BlockSpec((None, s, QH), lambda i, h: (i, 0, h)),
                  pl.BlockSpec((None, s, QH), lambda i, h: (i, 0, h)),
                  pl.BlockSpec((None, s, HEAD_DIM), lambda i, h: (i, 0, h))],
        out_specs=pl.BlockSpec((None, s, HEAD_DIM), lambda i, h: (i, 0, h)),
        out_shape=jax.ShapeDtypeStruct((b, s, MLA_HEADS * HEAD_DIM), F32),
        compiler_params=_params("arbitrary", "arbitrary"),
        name="flash_mla",
    )(q, k, v)


def _flash_fox(q, k, v, e, et, blk):
    b, s, _ = q.shape
    ek = et.reshape(b, FOX_HEADS, s // blk, blk)
    return pl.pallas_call(
        functools.partial(_flash_fox_kernel, blk=blk),
        grid=(b, FOX_HEADS),
        in_specs=[pl.BlockSpec((None, s, HEAD_DIM), lambda i, h: (i, 0, h)),
                  pl.BlockSpec((None, s, HEAD_DIM), lambda i, h: (i, 0, h // FOX_GROUP)),
                  pl.BlockSpec((None, s, HEAD_DIM), lambda i, h: (i, 0, h // FOX_GROUP)),
                  pl.BlockSpec((None, None, s // blk, blk), lambda i, h: (i, h, 0, 0)),
                  pl.BlockSpec((None, s, FOX_HEADS), lambda i, h: (i, 0, 0))],
        out_specs=pl.BlockSpec((None, s, HEAD_DIM), lambda i, h: (i, 0, h)),
        out_shape=jax.ShapeDtypeStruct((b, s, FOX_HEADS * HEAD_DIM), F32),
        compiler_params=_params("arbitrary", "arbitrary"),
        name="flash_fox",
    )(q, k, v, ek, e)


def _absorb_kernel(q_ref, w_ref, o_ref):
    o_ref[...] = lax.dot_general(q_ref[...], w_ref[...].astype(BF16),
                                 (((1,), (1,)), ((), ())),
                                 preferred_element_type=F32).astype(o_ref.dtype)


def _absorb(q256, w_uk2d):
    t = q256.shape[0]
    return pl.pallas_call(
        _absorb_kernel,
        grid=(MLA_HEADS,),
        in_specs=[pl.BlockSpec((t, HEAD_DIM), lambda h: (0, 2 * h)),
                  pl.BlockSpec((KV_LORA, HEAD_DIM), lambda h: (0, h))],
        out_specs=pl.BlockSpec((t, KV_LORA), lambda h: (0, h)),
        out_shape=jax.ShapeDtypeStruct((t, MLA_HEADS * KV_LORA), BF16),
        compiler_params=_params("arbitrary"),
        name="q_absorb",
    )(q256, w_uk2d)


def _value_up_kernel(o_ref_in, w_ref, o_ref):
    o_ref[...] = jnp.dot(o_ref_in[...].astype(BF16), w_ref[...].astype(BF16),
                         preferred_element_type=F32)


def _value_up(o_lat2d, w_uv2d):
    t = o_lat2d.shape[0]
    return pl.pallas_call(
        _value_up_kernel,
        grid=(MLA_HEADS,),
        in_specs=[pl.BlockSpec((t, KV_LORA), lambda h: (0, h)),
                  pl.BlockSpec((KV_LORA, HEAD_DIM), lambda h: (0, h))],
        out_specs=pl.BlockSpec((t, HEAD_DIM), lambda h: (0, h)),
        out_shape=jax.ShapeDtypeStruct((t, MLA_HEADS * HEAD_DIM), F32),
        compiler_params=_params("arbitrary"),
        name="value_up",
    )(o_lat2d, w_uv2d)


def _decode_kernel(pt_ref, qlat_ref, qrope_ref, fq_ref, ckvn_ref, krn_ref, fkn_ref, fvn_ref,
                   lfn_ref, lat_hbm, kr_hbm, fk_hbm, fv_hbm, lf_hbm,
                   olat_ref, ofox_ref,
                   lat_buf, kr_buf, fk_buf, fv_buf, lf_buf, sems,
                   m_m, l_m, acc_m, m_f, l_f, acc_f, carry, *, cp, nb, nch):
    b = pl.program_id(0)
    j = pl.program_id(1)
    total = nb * nch
    t = b * nch + j
    slot = t % DECODE_SLOTS
    kc = cp * PAGE_SIZE
    nt = (((1,), (1,)), ((), ()))
    bufs = (lat_buf, kr_buf, fk_buf, fv_buf, lf_buf)

    def start_chunk(step, sl):
        step = step % total
        bb = step // nch
        first_page = (nch - 1 - step % nch) * cp
        for p in range(cp):
            page = pt_ref[bb, first_page + p]
            dsts = (lat_buf.at[sl, p], kr_buf.at[sl, :, pl.ds(p * PAGE_SIZE, PAGE_SIZE)],
                    fk_buf.at[sl, p], fv_buf.at[sl, p], lf_buf.at[sl, p])
            for a, (hbm, dst) in enumerate(zip((lat_hbm, kr_hbm, fk_hbm, fv_hbm, lf_hbm), dsts)):
                pltpu.make_async_copy(hbm.at[page], dst, sems.at[a, sl]).start()

    def wait_chunk(sl):
        for a, buf in enumerate(bufs):
            pltpu.make_async_copy(buf.at[sl], buf.at[sl], sems.at[a, sl]).wait()

    @pl.when(t == 0)
    def _():
        for ahead in range(DECODE_SLOTS - 1):
            start_chunk(ahead, ahead)

    qlat = qlat_ref[...]
    qrope = qrope_ref[:, :ROPE_DIM]
    fq = fq_ref[...]
    head_col = lax.broadcasted_iota(jnp.int32, (FOX_HEADS, 1), 0)
    kv0 = head_col < FOX_GROUP

    @pl.when(j == 0)
    def _():
        ckvn = ckvn_ref[...].astype(BF16).astype(F32)
        krn = krn_ref[:, :ROPE_DIM].astype(BF16).astype(F32)
        m_m[...] = (jnp.sum(qlat.astype(F32) * ckvn, axis=-1, keepdims=True)
                    + jnp.sum(qrope.astype(F32) * krn, axis=-1, keepdims=True))
        l_m[...] = jnp.ones_like(l_m)
        acc_m[...] = jnp.broadcast_to(ckvn, acc_m.shape)
        fkn = fkn_ref[...].astype(BF16).astype(F32)
        fvn = fvn_ref[...].astype(BF16).astype(F32)
        fqf = fq.astype(F32)
        m_f[...] = jnp.where(kv0, jnp.sum(fqf * fkn[:, :HEAD_DIM], axis=-1, keepdims=True),
                             jnp.sum(fqf * fkn[:, HEAD_DIM:], axis=-1, keepdims=True))
        l_f[...] = jnp.ones_like(l_f)
        acc_f[...] = jnp.where(kv0, fvn[:, :HEAD_DIM], fvn[:, HEAD_DIM:])
        carry[...] = lfn_ref[...]

    wait_chunk(slot)
    start_chunk(t + DECODE_SLOTS - 1, (t + DECODE_SLOTS - 1) % DECODE_SLOTS)

    latb = lat_buf[slot].reshape(kc, KV_LORA).astype(BF16)
    krb = kr_buf[slot].astype(BF16)
    s = (lax.dot_general(qlat, latb, nt, preferred_element_type=F32)
         + jnp.dot(qrope, krb, preferred_element_type=F32))
    m_old = m_m[...]
    m_new = jnp.maximum(m_old, jnp.max(s, axis=-1, keepdims=True))
    alpha = jnp.exp(m_old - m_new)
    p = jnp.exp(s - m_new)
    l_m[...] = alpha * l_m[...] + jnp.sum(p, axis=-1, keepdims=True)
    acc_m[...] = alpha * acc_m[...] + jnp.dot(p.astype(BF16), latb, preferred_element_type=F32)
    m_m[...] = m_new

    lf = lf_buf[slot]
    x1, x2, x3 = _split3(lf.reshape(cp * FOX_HEADS, PAGE_SIZE))
    jr = lax.broadcasted_iota(jnp.int32, (PAGE_SIZE, PAGE_SIZE), 0)
    kcol = lax.broadcasted_iota(jnp.int32, (PAGE_SIZE, PAGE_SIZE), 1)
    later = (jr > kcol).astype(BF16)
    within = (jnp.dot(x1, later, preferred_element_type=F32)
              + jnp.dot(x2, later, preferred_element_type=F32)
              + jnp.dot(x3, later, preferred_element_type=F32))
    off = carry[...]
    tiles = [None] * cp
    for pg in range(cp - 1, -1, -1):
        tiles[pg] = within[pg * FOX_HEADS:(pg + 1) * FOX_HEADS, :] + off
        off = off + jnp.sum(lf[pg], axis=-1, keepdims=True)
    carry[...] = off
    bias = jnp.concatenate(tiles, axis=1)

    def kv_head(buf, n):
        rows = buf[slot, :, pl.ds(n, PAGE_SIZE, stride=FOX_KV_HEADS), :]
        return rows.reshape(kc, HEAD_DIM).astype(BF16)

    s = jnp.where(kv0, lax.dot_general(fq, kv_head(fk_buf, 0), nt, preferred_element_type=F32),
                  lax.dot_general(fq, kv_head(fk_buf, 1), nt, preferred_element_type=F32)) + bias
    m_old = m_f[...]
    m_new = jnp.maximum(m_old, jnp.max(s, axis=-1, keepdims=True))
    alpha = jnp.exp(m_old - m_new)
    p = jnp.exp(s - m_new)
    pb = p.astype(BF16)
    l_f[...] = alpha * l_f[...] + jnp.sum(p, axis=-1, keepdims=True)
    pv = jnp.where(kv0, jnp.dot(pb, kv_head(fv_buf, 0), preferred_element_type=F32),
                   jnp.dot(pb, kv_head(fv_buf, 1), preferred_element_type=F32))
    acc_f[...] = alpha * acc_f[...] + pv
    m_f[...] = m_new

    @pl.when(j == nch - 1)
    def _():
        olat_ref[...] = acc_m[...] / l_m[...]
        ofox_ref[...] = acc_f[...] / l_f[...]

    @pl.when(t == total - 1)
    def _():
        for ahead in range(1, DECODE_SLOTS):
            wait_chunk((t + ahead) % DECODE_SLOTS)


def _decode_attention(page_table, qlat, qrope, fq, ckv_new, kr_new, fk_new, fv_new, lf_new,
                      lat_c, kr_c, fk_c, fv_c, lf_c):
    nb, n_pages = page_table.shape
    cp = DECODE_PAGES
    ns = DECODE_SLOTS
    kvw = FOX_KV_HEADS * HEAD_DIM
    kv_rows = FOX_KV_HEADS * PAGE_SIZE
    per_b = lambda *shape: pl.BlockSpec((None,) + shape, lambda b, j, pt: (b, 0, 0))
    hbm = pl.BlockSpec(memory_space=pl.ANY)
    grid_spec = pltpu.PrefetchScalarGridSpec(
        num_scalar_prefetch=1,
        grid=(nb, n_pages // cp),
        in_specs=[per_b(MLA_HEADS, KV_LORA), per_b(MLA_HEADS, LANES), per_b(FOX_HEADS, HEAD_DIM),
                  per_b(1, KV_LORA), per_b(1, LANES), per_b(1, kvw), per_b(1, kvw),
                  per_b(FOX_HEADS, 1), hbm, hbm, hbm, hbm, hbm],
        out_specs=[per_b(MLA_HEADS, KV_LORA), per_b(FOX_HEADS, HEAD_DIM)],
        scratch_shapes=[
            pltpu.VMEM((ns, cp, PAGE_SIZE, KV_LORA), F32),
            pltpu.VMEM((ns, ROPE_DIM, cp * PAGE_SIZE), F32),
            pltpu.VMEM((ns, cp, kv_rows, HEAD_DIM), F32),
            pltpu.VMEM((ns, cp, kv_rows, HEAD_DIM), F32),
            pltpu.VMEM((ns, cp, FOX_HEADS, PAGE_SIZE), F32),
            pltpu.SemaphoreType.DMA((5, ns)),
            pltpu.VMEM((MLA_HEADS, 1), F32), pltpu.VMEM((MLA_HEADS, 1), F32),
            pltpu.VMEM((MLA_HEADS, KV_LORA), F32),
            pltpu.VMEM((FOX_HEADS, 1), F32), pltpu.VMEM((FOX_HEADS, 1), F32),
            pltpu.VMEM((FOX_HEADS, HEAD_DIM), F32),
            pltpu.VMEM((FOX_HEADS, 1), F32),
        ],
    )
    return pl.pallas_call(
        functools.partial(_decode_kernel, cp=cp, nb=nb, nch=n_pages // cp),
        grid_spec=grid_spec,
        out_shape=[jax.ShapeDtypeStruct((nb, MLA_HEADS, KV_LORA), F32),
                   jax.ShapeDtypeStruct((nb, FOX_HEADS, HEAD_DIM), F32)],
        compiler_params=_params("arbitrary", "arbitrary"),
        name="decode_attention",
    )(page_table, qlat, qrope, fq, ckv_new, kr_new, fk_new, fv_new, lf_new,
      lat_c, kr_c, fk_c, fv_c, lf_c)


def _swap_halves(x):
    half = x.shape[-1] // 2
    return jnp.concatenate([x[..., half:], x[..., :half]], axis=-1)


def _relayout_w_in(w):
    o = 0
    segs = {}
    for name, width in (("cq", Q_LORA), ("ckv", KV_LORA), ("kr", ROPE_DIM),
                        ("fq", FOX_HEADS * HEAD_DIM), ("fk", FOX_KV_HEADS * HEAD_DIM),
                        ("fv", FOX_KV_HEADS * HEAD_DIM), ("ff", FOX_HEADS)):
        segs[name] = w[:, o:o + width]
        o += width
    zeros = lambda n: jnp.zeros((w.shape[0], n), w.dtype)
    cols = [segs["cq"], segs["ckv"], segs["fq"], segs["fk"], segs["fv"],
            segs["kr"], _swap_halves(segs["kr"]), segs["ff"], zeros(LANES - FOX_HEADS),
            zeros(Z_WIDTH - Z_FF - LANES)]
    return jnp.concatenate(cols, axis=1).astype(BF16)


def _relayout_w_q_up(w):
    nope, rope = w[..., :HEAD_DIM], w[..., HEAD_DIM:]
    ext = jnp.concatenate([nope, rope, _swap_halves(rope)], axis=-1)
    return ext.reshape(w.shape[0], MLA_HEADS * QH).astype(BF16)


def _rope_tables(pos):
    half = ROPE_DIM // 2
    inv_freq = jnp.power(ROPE_THETA, -jnp.arange(half, dtype=F32) * 2.0 / ROPE_DIM)
    ang = pos.astype(F32)[:, None] * inv_freq[None, :]
    cos, sin = jnp.cos(ang), jnp.sin(ang)
    z = jnp.zeros((pos.shape[0], LANES - ROPE_DIM), F32)
    return (jnp.concatenate([cos, cos, z], axis=1), jnp.concatenate([-sin, sin, z], axis=1))


def _layer_front(x, mods, rows_per_batch, pos_tables, pos_blocks, tm, w, attend, extra_rows=0):
    sh1, sc1, ga1, sh2, sc2, _ = [_Mod(m, rows_per_batch) for m in mods]
    cs1, cs2 = pos_tables
    tm_small = min(tm, 256)
    h = _norm_mod(x, w["g_attn"], sc1, sh1, tm_small)
    z = _dense(h, w["w_in"], tm=tm, tn=512, out_dtype=F32, name="in_proj")
    (cqn, lat, kva, kr, fq, fk, fv, fkb, fvb, lf, lf128) = _post_in(
        z, w["g_q"], w["g_kv"], w["b_f"], cs1, cs2, tm_small, pos_blocks * (tm // tm_small) if pos_blocks > 1 else 1)
    pos_spec = pl.BlockSpec((tm, LANES), lambda i, j: (i % pos_blocks, 0))
    q256 = _dense(cqn, w["w_q"], tm=tm, tn=512, out_dtype=BF16, name="q_up",
                  epi=_epi_q_rope, extras=((cs1, pos_spec), (cs2, pos_spec)))
    ym, yf = attend(q256, kva, fq, fkb, fvb, lf128, lat, kr, fk, fv, lf)
    y = _norm2(ym, yf, w["g_mla"], w["g_fox"], tm_small)
    x = _dense(y, w["w_out"], tm=tm, tn=512, out_dtype=F32, name="out_proj", epi=_epi_residual,
               extras=((x, pl.BlockSpec((tm, 512), lambda i, j: (i, j))), (ga1.arr, ga1.spec(tm, 512))))
    h = _norm_mod(x, w["g_ffn"], sc2, sh2, math.gcd(tm_small, extra_rows) if extra_rows else tm_small,
                  extra_rows)
    return x, h, (lat, kr, fk, fv, lf)


def _ffn_down(a, first_row, x, ga2, rows_per_batch, w_down, tm):
    ga2 = _Mod(ga2, rows_per_batch)
    tn = 512
    return _dense(a, w_down, tm=tm, tn=tn, out_dtype=F32, name="ffn_down", epi=_epi_residual,
                  rows=x.shape[0], first_row=first_row,
                  extras=((x, pl.BlockSpec((tm, tn), lambda i, j: (i, j))), (ga2.arr, ga2.spec(tm, tn))))


def kernel(x_prompt, x_sample, c_prompt, c_sample, cache_mla_latent, cache_mla_krope, cache_fox_k,
           cache_fox_v, cache_fox_logf, page_table, w_ada, b_ada, g_attn_norm, w_in, g_q_norm, w_q_up,
           g_kv_norm, w_uk, w_uv, b_forget, g_mla_out, g_fox_out, w_out, g_ffn_norm, w_gate, w_up,
           w_down, g_final):
    nbp, seq, d = x_prompt.shape
    nbs = x_sample.shape[0]
    depth = w_ada.shape[0]
    n_phys = cache_mla_latent.shape[1]
    past = page_table.shape[1] * PAGE_SIZE
    kvw = FOX_KV_HEADS * HEAD_DIM

    xp = x_prompt.reshape(nbp * seq, d)
    xs = x_sample.reshape(nbs, d)
    c_all = jnp.concatenate([c_prompt, c_sample], axis=0)
    tables_p = _rope_tables(jnp.arange(seq, dtype=jnp.int32))
    tables_s = tuple(jnp.broadcast_to(t, (nbs, LANES))
                     for t in _rope_tables(past + jnp.arange(1, dtype=jnp.int32)))
    tm_p = 1024
    st_p, st_s = [], []
    for l in range(depth):
        w = dict(g_attn=g_attn_norm[l], w_in=_relayout_w_in(w_in[l]), g_q=g_q_norm[l],
                 w_q=_relayout_w_q_up(w_q_up[l]), g_kv=g_kv_norm[l], b_f=b_forget[l],
                 g_mla=g_mla_out[l], g_fox=g_fox_out[l], w_out=w_out[l], g_ffn=g_ffn_norm[l],
                 w_gate=w_gate[l], w_up=w_up[l], w_down=w_down[l].astype(BF16))
        w_uk2d = w_uk[l].reshape(KV_LORA, MLA_HEADS * HEAD_DIM)
        w_uv2d = w_uv[l].reshape(KV_LORA, MLA_HEADS * HEAD_DIM)
        mod = _ada(c_all, w_ada[l], b_ada[l])
        mods_p = [mod[:nbp, i * d:(i + 1) * d] for i in range(N_MOD)]
        mods_s = [mod[nbp:, i * d:(i + 1) * d] for i in range(N_MOD)]

        def attend_prompt(q256, kva, fq, fkb, fvb, lf128, *_):
            k256, v = _kv_up(kva, w_uk2d, w_uv2d, tm_p)
            ym = _flash_mla(q256.reshape(nbp, seq, -1), k256.reshape(nbp, seq, -1),
                            v.reshape(nbp, seq, -1), FLASH_BLOCK)
            e, et = _suffix(lf128.reshape(nbp, seq, LANES))
            yf = _flash_fox(fq.reshape(nbp, seq, -1), fkb.reshape(nbp, seq, kvw),
                            fvb.reshape(nbp, seq, kvw), e, et, FLASH_BLOCK)
            return ym.reshape(nbp * seq, -1), yf.reshape(nbp * seq, -1)

        def attend_sample(q256, kva, fq, fkb, fvb, lf128, lat, kr, fk, fv, lf):
            qlat = _absorb(q256, w_uk2d).reshape(nbs, MLA_HEADS, KV_LORA)
            qrope = q256.reshape(nbs, MLA_HEADS, QH)[:, :, LANES:]
            o_lat, o_fox = _decode_attention(
                page_table, qlat, qrope, fq.reshape(nbs, FOX_HEADS, HEAD_DIM),
                lat.reshape(nbs, 1, KV_LORA), kva[:, KV_LORA:].astype(F32).reshape(nbs, 1, LANES),
                fk.reshape(nbs, 1, kvw), fv.reshape(nbs, 1, kvw), lf.reshape(nbs, FOX_HEADS, 1),
                cache_mla_latent[l], jnp.swapaxes(cache_mla_krope[l], 1, 2),
                cache_fox_k[l].reshape(n_phys, PAGE_SIZE * FOX_KV_HEADS, HEAD_DIM),
                cache_fox_v[l].reshape(n_phys, PAGE_SIZE * FOX_KV_HEADS, HEAD_DIM),
                jnp.swapaxes(cache_fox_logf[l], 1, 2))
            ym = _value_up(o_lat.reshape(nbs, MLA_HEADS * KV_LORA), w_uv2d)
            return ym, o_fox.reshape(nbs, FOX_HEADS * HEAD_DIM)

        xs, hs, ss = _layer_front(xs, mods_s, 1, tables_s, 1, nbs, w, attend_sample)
        xp, hp, sp = _layer_front(xp, mods_p, seq, tables_p, seq // tm_p, tm_p, w, attend_prompt,
                                  extra_rows=nbs)
        n_p = nbp * seq
        h_all = _place_rows(hp, hs)
        a_all = _gate_up(h_all, w["w_gate"], w["w_up"], (n_p + nbs) // GATE_UP_ROW_TILES, 256)
        xp = _ffn_down(a_all, 0, xp, mods_p[5], seq, w["w_down"], 512)
        xs = _ffn_down(a_all, n_p, xs, mods_s[5], 1, w["w_down"], nbs)
        st_p.append(sp)
        st_s.append(ss)

    y_prompt = _norm(xp, g_final, 256).reshape(nbp, seq, d)
    y_sample = _norm(xs, g_final, nbs).reshape(nbs, 1, d)

    def stack(states, idx, shape):
        return jnp.stack([s[idx].reshape(shape) for s in states])

    outs = [y_prompt, y_sample]
    for states, lead in ((st_p, (nbp, seq)), (st_s, (nbs, 1))):
        outs += [stack(states, 0, lead + (KV_LORA,)), stack(states, 1, lead + (ROPE_DIM,)),
                 stack(states, 2, lead + (FOX_KV_HEADS, HEAD_DIM)),
                 stack(states, 3, lead + (FOX_KV_HEADS, HEAD_DIM)),
                 stack(states, 4, lead + (FOX_HEADS,))]
    return tuple(outs)
```

```python
import functools
import math

import jax
import jax.numpy as jnp
from jax import lax
from jax.experimental import pallas as pl
from jax.experimental.pallas import tpu as pltpu

F32 = jnp.float32
BF16 = jnp.bfloat16

D_MODEL = 4096
HEAD_DIM = 128
MLA_HEADS = 16
FOX_HEADS = 16
FOX_KV_HEADS = 2
FOX_GROUP = FOX_HEADS // FOX_KV_HEADS
Q_LORA = 1536
KV_LORA = 512
ROPE_DIM = 64
ROPE_THETA = 10000.0
MLA_SCALE = (HEAD_DIM + ROPE_DIM) ** -0.5
FOX_SCALE = HEAD_DIM ** -0.5
N_MOD = 6
PAGE_SIZE = 128
EPS = 1e-6

LANES = 128
VMEM_LIMIT = 56 * 1024 * 1024

Z_CQ = 0
Z_CKV = Z_CQ + Q_LORA
Z_FQ = Z_CKV + KV_LORA
Z_FK = Z_FQ + FOX_HEADS * HEAD_DIM
Z_FV = Z_FK + FOX_KV_HEADS * HEAD_DIM
Z_KR = Z_FV + FOX_KV_HEADS * HEAD_DIM
Z_FF = Z_KR + LANES
Z_WIDTH = 5120
KVA_WIDTH = KV_LORA + LANES
QH = 2 * LANES
FLASH_BLOCK = 1024
GATE_UP_ROW_TILES = 8
DECODE_PAGES = 16
DECODE_SLOTS = 3


def _params(*sem):
    return pltpu.CompilerParams(dimension_semantics=sem, vmem_limit_bytes=VMEM_LIMIT)


def _rms(x):
    return x * lax.rsqrt(jnp.mean(x * x, axis=-1, keepdims=True) + EPS)


def _rope_lanes(g, cs1, cs2):
    return g * cs1 + pltpu.roll(g, ROPE_DIM, axis=1) * cs2


def _ada_kernel(c_ref, w_ref, b_ref, o_ref):
    c = c_ref[...]
    a = (c * jax.nn.sigmoid(c)).astype(BF16)
    o_ref[...] = jnp.dot(a, w_ref[...].astype(BF16), preferred_element_type=F32) + b_ref[...]


def _ada(c, w, b):
    m, k = c.shape
    n = w.shape[1]
    tn = 512
    return pl.pallas_call(
        _ada_kernel,
        grid=(n // tn,),
        in_specs=[pl.BlockSpec((m, k), lambda j: (0, 0)),
                  pl.BlockSpec((k, tn), lambda j: (0, j)),
                  pl.BlockSpec((1, tn), lambda j: (0, j))],
        out_specs=pl.BlockSpec((m, tn), lambda j: (0, j)),
        out_shape=jax.ShapeDtypeStruct((m, n), F32),
        compiler_params=_params("arbitrary"),
        name="ada",
    )(c, w, b.reshape(1, n))


class _Mod:
    def __init__(self, arr, rows_per_batch):
        self.rows_per_batch = rows_per_batch
        if rows_per_batch == 1:
            self.arr = arr
        else:
            self.arr = arr.reshape(arr.shape[0], 1, arr.shape[1])

    def spec(self, tm, tn, row_only=False, clamp=lambda i: i):
        if self.rows_per_batch == 1:
            if row_only:
                return pl.BlockSpec((tm, tn), lambda i: (clamp(i), 0))
            return pl.BlockSpec((tm, tn), lambda i, j: (i, j))
        per = self.rows_per_batch // tm
        if row_only:
            return pl.BlockSpec((None, 1, tn), lambda i: (clamp(i) // per, 0, 0))
        return pl.BlockSpec((None, 1, tn), lambda i, j: (i // per, 0, j))


def _norm_mod_kernel(x_ref, g_ref, sc_ref, sh_ref, o_ref):
    y = _rms(x_ref[...]) * g_ref[...]
    o_ref[...] = (y * (1.0 + sc_ref[...]) + sh_ref[...]).astype(o_ref.dtype)


def _norm_kernel(x_ref, g_ref, o_ref):
    o_ref[...] = (_rms(x_ref[...]) * g_ref[...]).astype(o_ref.dtype)


def _norm_mod_pad_kernel(x_ref, g_ref, sc_ref, sh_ref, o_ref, *, n_real):
    @pl.when(pl.program_id(0) < n_real)
    def _():
        _norm_mod_kernel(x_ref, g_ref, sc_ref, sh_ref, o_ref)

    @pl.when(pl.program_id(0) >= n_real)
    def _():
        o_ref[...] = jnp.zeros_like(o_ref)


def _norm_mod(x, g, sc, sh, tm, extra_rows=0):
    t, d = x.shape
    assert t % tm == 0 and extra_rows % tm == 0
    n_real = t // tm
    clamp = lambda i: jnp.minimum(i, n_real - 1)
    body = _norm_mod_kernel if not extra_rows else functools.partial(_norm_mod_pad_kernel, n_real=n_real)
    return pl.pallas_call(
        body,
        grid=(n_real + extra_rows // tm,),
        in_specs=[pl.BlockSpec((tm, d), lambda i: (clamp(i), 0)),
                  pl.BlockSpec((1, d), lambda i: (0, 0)),
                  sc.spec(tm, d, row_only=True, clamp=clamp),
                  sh.spec(tm, d, row_only=True, clamp=clamp)],
        out_specs=pl.BlockSpec((tm, d), lambda i: (i, 0)),
        out_shape=jax.ShapeDtypeStruct((t + extra_rows, d), BF16),
        compiler_params=_params("arbitrary"),
        name="norm_mod",
    )(x, g.reshape(1, d), sc.arr, sh.arr)


def _place_rows_kernel(dst_ref, src_ref, o_ref):
    del dst_ref
    o_ref[...] = src_ref[...]


def _place_rows(dst, src):
    n, d = src.shape
    assert dst.shape[0] % n == 0
    last = dst.shape[0] // n - 1
    return pl.pallas_call(
        _place_rows_kernel,
        grid=(1,),
        in_specs=[pl.BlockSpec(memory_space=pl.ANY), pl.BlockSpec((n, d), lambda i: (0, 0))],
        out_specs=pl.BlockSpec((n, d), lambda i: (last, 0)),
        out_shape=jax.ShapeDtypeStruct(dst.shape, dst.dtype),
        input_output_aliases={0: 0},
        compiler_params=_params("arbitrary"),
        name="place_rows",
    )(dst, src)


def _norm(x, g, tm):
    t, d = x.shape
    return pl.pallas_call(
        _norm_kernel,
        grid=(t // tm,),
        in_specs=[pl.BlockSpec((tm, d), lambda i: (i, 0)),
                  pl.BlockSpec((1, d), lambda i: (0, 0))],
        out_specs=pl.BlockSpec((tm, d), lambda i: (i, 0)),
        out_shape=jax.ShapeDtypeStruct((t, d), F32),
        compiler_params=_params("arbitrary"),
        name="final_norm",
    )(x, g.reshape(1, d))


def _norm2_kernel(ym_ref, yf_ref, gm_ref, gf_ref, o_ref):
    w = ym_ref.shape[1]
    o_ref[:, :w] = (_rms(ym_ref[...].astype(F32)) * gm_ref[...]).astype(o_ref.dtype)
    o_ref[:, w:] = (_rms(yf_ref[...].astype(F32)) * gf_ref[...]).astype(o_ref.dtype)


def _norm2(ym, yf, gm, gf, tm):
    t, w = ym.shape
    return pl.pallas_call(
        _norm2_kernel,
        grid=(t // tm,),
        in_specs=[pl.BlockSpec((tm, w), lambda i: (i, 0)),
                  pl.BlockSpec((tm, w), lambda i: (i, 0)),
                  pl.BlockSpec((1, w), lambda i: (0, 0)),
                  pl.BlockSpec((1, w), lambda i: (0, 0))],
        out_specs=pl.BlockSpec((tm, 2 * w), lambda i: (i, 0)),
        out_shape=jax.ShapeDtypeStruct((t, 2 * w), BF16),
        compiler_params=_params("arbitrary"),
        name="mix_norm",
    )(ym, yf, gm.reshape(1, w), gf.reshape(1, w))


def _mm_kernel(a_ref, w_ref, *rest, epi, n_extra):
    o_ref = rest[n_extra]
    acc = jnp.dot(a_ref[...].astype(BF16), w_ref[...].astype(BF16),
                  preferred_element_type=F32)
    if epi is not None:
        acc = epi(acc, *rest[:n_extra])
    o_ref[...] = acc.astype(o_ref.dtype)


def _dense(a, w, *, tm, tn, out_dtype, name, epi=None, extras=(), rows=None, first_row=0):
    k = a.shape[1]
    m = a.shape[0] if rows is None else rows
    n = w.shape[1]
    assert m % tm == 0 and n % tn == 0 and first_row % tm == 0
    blk0 = first_row // tm
    return pl.pallas_call(
        functools.partial(_mm_kernel, epi=epi, n_extra=len(extras)),
        grid=(m // tm, n // tn),
        in_specs=[pl.BlockSpec((tm, k), lambda i, j: (i + blk0, 0)),
                  pl.BlockSpec((k, tn), lambda i, j: (0, j))] + [s for _, s in extras],
        out_specs=pl.BlockSpec((tm, tn), lambda i, j: (i, j)),
        out_shape=jax.ShapeDtypeStruct((m, n), out_dtype),
        compiler_params=_params("arbitrary", "arbitrary"),
        name=name,
    )(a, w, *[x for x, _ in extras])


def _epi_residual(acc, x_ref, ga_ref):
    return x_ref[...] + ga_ref[...] * acc


def _epi_q_rope(acc, cs1_ref, cs2_ref):
    cs1 = cs1_ref[...]
    cs2 = cs2_ref[...]
    parts = []
    for h in range(acc.shape[1] // QH):
        parts.append(acc[:, h * QH:h * QH + LANES])
        parts.append(_rope_lanes(acc[:, h * QH + LANES:(h + 1) * QH], cs1, cs2))
    return jnp.concatenate(parts, axis=1) * MLA_SCALE


def _gate_up_kernel(a_ref, wg_ref, wu_ref, o_ref):
    a = a_ref[...]
    g = jnp.dot(a, wg_ref[...].astype(BF16), preferred_element_type=F32)
    u = jnp.dot(a, wu_ref[...].astype(BF16), preferred_element_type=F32)
    o_ref[...] = (g * jax.nn.sigmoid(g) * u).astype(o_ref.dtype)


def _gate_up(a, wg, wu, tm, tn):
    m, k = a.shape
    n = wg.shape[1]
    assert m % tm == 0 and n % tn == 0
    return pl.pallas_call(
        _gate_up_kernel,
        grid=(m // tm, n // tn),
        in_specs=[pl.BlockSpec((tm, k), lambda i, j: (i, 0)),
                  pl.BlockSpec((k, tn), lambda i, j: (0, j)),
                  pl.BlockSpec((k, tn), lambda i, j: (0, j))],
        out_specs=pl.BlockSpec((tm, tn), lambda i, j: (i, j)),
        out_shape=jax.ShapeDtypeStruct((m, n), BF16),
        compiler_params=_params("arbitrary", "arbitrary"),
        name="ffn_gate_up",
    )(a, wg, wu)


def _post_in_kernel(z_ref, gq_ref, gkv_ref, bf_ref, cs1_ref, cs2_ref,
                    cqn_ref, lat_ref, kva_ref, kr_ref, fq_ref, fk_ref, fv_ref,
                    fkb_ref, fvb_ref, lf_ref, lf128_ref):
    cqn_ref[...] = (_rms(z_ref[:, Z_CQ:Z_CKV]) * gq_ref[...]).astype(BF16)
    lat = _rms(z_ref[:, Z_CKV:Z_FQ]) * gkv_ref[...]
    lat_ref[...] = lat
    kr = _rope_lanes(z_ref[:, Z_KR:Z_FF], cs1_ref[...], cs2_ref[...])
    kr_ref[...] = kr[:, :ROPE_DIM]
    kva_ref[:, :KV_LORA] = lat.astype(BF16)
    kva_ref[:, KV_LORA:] = kr.astype(BF16)
    fq_ref[...] = (z_ref[:, Z_FQ:Z_FK] * FOX_SCALE).astype(BF16)
    fk = z_ref[:, Z_FK:Z_FV]
    fv = z_ref[:, Z_FV:Z_KR]
    fk_ref[...] = fk
    fv_ref[...] = fv
    fkb_ref[...] = fk.astype(BF16)
    fvb_ref[...] = fv.astype(BF16)
    x = z_ref[:, Z_FF:Z_FF + LANES] + bf_ref[...]
    lf = jnp.minimum(x, 0.0) - jnp.log1p(jnp.exp(-jnp.abs(x)))
    lane = lax.broadcasted_iota(jnp.int32, lf.shape, 1)
    lf = jnp.where(lane < FOX_HEADS, lf, 0.0)
    lf128_ref[...] = lf
    lf_ref[...] = lf[:, :FOX_HEADS]


def _post_in(z, g_q, g_kv, b_f, cs1, cs2, tm, pos_blocks):
    t = z.shape[0]
    kvw = FOX_KV_HEADS * HEAD_DIM
    row = lambda w: pl.BlockSpec((tm, w), lambda i: (i, 0))
    const = lambda w: pl.BlockSpec((1, w), lambda i: (0, 0))
    pos = pl.BlockSpec((tm, LANES), lambda i: (i % pos_blocks, 0))
    bf = jnp.zeros((1, LANES), F32).at[0, :FOX_HEADS].set(b_f)
    shapes = [((t, Q_LORA), BF16), ((t, KV_LORA), F32), ((t, KVA_WIDTH), BF16),
              ((t, ROPE_DIM), F32), ((t, FOX_HEADS * HEAD_DIM), BF16),
              ((t, kvw), F32), ((t, kvw), F32), ((t, kvw), BF16), ((t, kvw), BF16),
              ((t, FOX_HEADS), F32), ((t, LANES), F32)]
    return pl.pallas_call(
        _post_in_kernel,
        grid=(t // tm,),
        in_specs=[row(Z_WIDTH), const(Q_LORA), const(KV_LORA), const(LANES), pos, pos],
        out_specs=[row(s[1]) for s, _ in shapes],
        out_shape=[jax.ShapeDtypeStruct(s, dt) for s, dt in shapes],
        compiler_params=_params("arbitrary"),
        name="post_in",
    )(z, g_q.reshape(1, Q_LORA), g_kv.reshape(1, KV_LORA), bf, cs1, cs2)


def _split3(x):
    a1 = x.astype(BF16)
    r1 = x - a1.astype(F32)
    a2 = r1.astype(BF16)
    a3 = (r1 - a2.astype(F32)).astype(BF16)
    return a1, a2, a3


def _suffix_kernel(lf_ref, e_ref, et_ref, *, rows):
    s = lf_ref.shape[0]
    parts = _split3(lf_ref[...])
    blocks = []
    for r0 in range(0, s, rows):
        row = lax.broadcasted_iota(jnp.int32, (rows, s), 0) + r0
        col = lax.broadcasted_iota(jnp.int32, (rows, s), 1)
        upper = (col > row).astype(BF16)
        blocks.append(sum(jnp.dot(upper, a, preferred_element_type=F32) for a in parts))
    e = jnp.concatenate(blocks, axis=0)
    e_ref[...] = e[:, :FOX_HEADS]
    et_ref[...] = e.T[:FOX_HEADS, :]


def _suffix(lf128):
    b, s, _ = lf128.shape
    return pl.pallas_call(
        functools.partial(_suffix_kernel, rows=512),
        grid=(b,),
        in_specs=[pl.BlockSpec((None, s, LANES), lambda i: (i, 0, 0))],
        out_specs=[pl.BlockSpec((None, s, FOX_HEADS), lambda i: (i, 0, 0)),
                   pl.BlockSpec((None, FOX_HEADS, s), lambda i: (i, 0, 0))],
        out_shape=[jax.ShapeDtypeStruct((b, s, FOX_HEADS), F32),
                   jax.ShapeDtypeStruct((b, FOX_HEADS, s), F32)],
        compiler_params=_params("arbitrary"),
        name="suffix_logf",
    )(lf128)


def _flash(q_ref, k_ref, v_ref, o_ref, ek_ref, eq, *, blk):
    s_len = q_ref.shape[0]
    dv = v_ref.shape[1]
    row = lax.broadcasted_iota(jnp.int32, (blk, blk), 0)
    col = lax.broadcasted_iota(jnp.int32, (blk, blk), 1)
    causal = row >= col
    for qi in range(s_len // blk):
        q = q_ref[qi * blk:(qi + 1) * blk, :]
        eq_blk = None if eq is None else eq[qi * blk:(qi + 1) * blk, :]

        def step(ki, carry, diagonal):
            m, l, acc = carry
            k0 = ki * blk if diagonal else pl.multiple_of(ki * blk, blk)
            k = k_ref[pl.ds(k0, blk), :]
            v = v_ref[pl.ds(k0, blk), :]
            s = lax.dot_general(q, k, (((1,), (1,)), ((), ())), preferred_element_type=F32)
            if ek_ref is not None:
                s = s + ek_ref[pl.ds(ki, 1), :] - eq_blk
            if diagonal:
                s = jnp.where(causal, s, -jnp.inf)
            m_new = jnp.maximum(m, jnp.max(s, axis=-1, keepdims=True))
            alpha = jnp.exp(m - m_new)
            p = jnp.exp(s - m_new)
            l = alpha * l + jnp.sum(p, axis=-1, keepdims=True)
            acc = alpha * acc + jnp.dot(p.astype(BF16), v, preferred_element_type=F32)
            return m_new, l, acc

        carry = (jnp.full((blk, 1), -jnp.inf, F32), jnp.zeros((blk, 1), F32),
                 jnp.zeros((blk, dv), F32))
        if qi > 0:
            carry = lax.fori_loop(0, qi, lambda ki, c: step(ki, c, False), carry)
        _, l, acc = step(qi, carry, True)
        o_ref[qi * blk:(qi + 1) * blk, :] = (acc / l).astype(o_ref.dtype)


def _flash_mla_kernel(q_ref, k_ref, v_ref, o_ref, *, blk):
    _flash(q_ref, k_ref, v_ref, o_ref, None, None, blk=blk)


def _flash_fox_kernel(q_ref, k_ref, v_ref, ek_ref, e_ref, o_ref, *, blk):
    h = pl.program_id(1)
    e = e_ref[...]
    lane = lax.broadcasted_iota(jnp.int32, e.shape, 1)
    eq = jnp.sum(jnp.where(lane == h, e, 0.0), axis=-1, keepdims=True)
    _flash(q_ref, k_ref, v_ref, o_ref, ek_ref, eq, blk=blk)


def _kv_up_kernel(a_ref, wk_ref, wv_ref, k_ref, v_ref):
    c = a_ref[:, :KV_LORA]
    kr = a_ref[:, KV_LORA:]
    kn = jnp.dot(c, wk_ref[...].astype(BF16), preferred_element_type=F32).astype(BF16)
    parts = []
    for h in range(kn.shape[1] // HEAD_DIM):
        parts += [kn[:, h * HEAD_DIM:(h + 1) * HEAD_DIM], kr]
    k_ref[...] = jnp.concatenate(parts, axis=1)
    v_ref[...] = jnp.dot(c, wv_ref[...].astype(BF16), preferred_element_type=F32).astype(BF16)


def _kv_up(kva, w_uk2d, w_uv2d, tm):
    t = kva.shape[0]
    heads = 2
    tn = heads * HEAD_DIM
    return pl.pallas_call(
        _kv_up_kernel,
        grid=(t // tm, MLA_HEADS // heads),
        in_specs=[pl.BlockSpec((tm, KVA_WIDTH), lambda i, j: (i, 0)),
                  pl.BlockSpec((KV_LORA, tn), lambda i, j: (0, j)),
                  pl.BlockSpec((KV_LORA, tn), lambda i, j: (0, j))],
        out_specs=[pl.BlockSpec((tm, heads * QH), lambda i, j: (i, j)),
                   pl.BlockSpec((tm, tn), lambda i, j: (i, j))],
        out_shape=[jax.ShapeDtypeStruct((t, MLA_HEADS * QH), BF16),
                   jax.ShapeDtypeStruct((t, MLA_HEADS * HEAD_DIM), BF16)],
        compiler_params=_params("arbitrary", "arbitrary"),
        name="kv_up",
    )(kva, w_uk2d, w_uv2d)


def _flash_mla(q, k, v, blk):
    b, s, _ = q.shape
    return pl.pallas_call(
        functools.partial(_flash_mla_kernel, blk=blk),
        grid=(b, MLA_HEADS),
        in_specs=[pl.BlockSpec((None, s, QH), lambda i, h: (i, 0, h)),
                  pl.BlockSpec((None, s, QH), lambda i, h: (i, 0, h)),
                  pl.BlockSpec((None, s, HEAD_DIM), lambda i, h: (i, 0, h))],
        out_specs=pl.BlockSpec((None, s, HEAD_DIM), lambda i, h: (i, 0, h)),
        out_shape=jax.ShapeDtypeStruct((b, s, MLA_HEADS * HEAD_DIM), BF16),
        compiler_params=_params("arbitrary", "arbitrary"),
        name="flash_mla",
    )(q, k, v)


def _flash_fox(q, k, v, e, et, blk):
    b, s, _ = q.shape
    ek = et.reshape(b, FOX_HEADS, s // blk, blk)
    return pl.pallas_call(
        functools.partial(_flash_fox_kernel, blk=blk),
        grid=(b, FOX_HEADS),
        in_specs=[pl.BlockSpec((None, s, HEAD_DIM), lambda i, h: (i, 0, h)),
                  pl.BlockSpec((None, s, HEAD_DIM), lambda i, h: (i, 0, h // FOX_GROUP)),
                  pl.BlockSpec((None, s, HEAD_DIM), lambda i, h: (i, 0, h // FOX_GROUP)),
                  pl.BlockSpec((None, None, s // blk, blk), lambda i, h: (i, h, 0, 0)),
                  pl.BlockSpec((None, s, FOX_HEADS), lambda i, h: (i, 0, 0))],
        out_specs=pl.BlockSpec((None, s, HEAD_DIM), lambda i, h: (i, 0, h)),
        out_shape=jax.ShapeDtypeStruct((b, s, FOX_HEADS * HEAD_DIM), BF16),
        compiler_params=_params("arbitrary", "arbitrary"),
        name="flash_fox",
    )(q, k, v, ek, e)


def _absorb_kernel(q_ref, w_ref, o_ref):
    o_ref[...] = lax.dot_general(q_ref[...], w_ref[...].astype(BF16),
                                 (((1,), (1,)), ((), ())),
                                 preferred_element_type=F32).astype(o_ref.dtype)


def _absorb(q256, w_uk2d):
    t = q256.shape[0]
    return pl.pallas_call(
        _absorb_kernel,
        grid=(MLA_HEADS,),
        in_specs=[pl.BlockSpec((t, HEAD_DIM), lambda h: (0, 2 * h)),
                  pl.BlockSpec((KV_LORA, HEAD_DIM), lambda h: (0, h))],
        out_specs=pl.BlockSpec((t, KV_LORA), lambda h: (0, h)),
        out_shape=jax.ShapeDtypeStruct((t, MLA_HEADS * KV_LORA), BF16),
        compiler_params=_params("arbitrary"),
        name="q_absorb",
    )(q256, w_uk2d)


def _value_up_kernel(o_ref_in, w_ref, o_ref):
    o_ref[...] = jnp.dot(o_ref_in[...].astype(BF16), w_ref[...].astype(BF16),
                         preferred_element_type=F32)


def _value_up(o_lat2d, w_uv2d):
    t = o_lat2d.shape[0]
    return pl.pallas_call(
        _value_up_kernel,
        grid=(MLA_HEADS,),
        in_specs=[pl.BlockSpec((t, KV_LORA), lambda h: (0, h)),
                  pl.BlockSpec((KV_LORA, HEAD_DIM), lambda h: (0, h))],
        out_specs=pl.BlockSpec((t, HEAD_DIM), lambda h: (0, h)),
        out_shape=jax.ShapeDtypeStruct((t, MLA_HEADS * HEAD_DIM), F32),
        compiler_params=_params("arbitrary"),
        name="value_up",
    )(o_lat2d, w_uv2d)


def _decode_kernel(pt_ref, qlat_ref, qrope_ref, fq_ref, ckvn_ref, krn_ref, fkn_ref, fvn_ref,
                   lfn_ref, lat_hbm, kr_hbm, fk_hbm, fv_hbm, lf_hbm,
                   olat_ref, ofox_ref,
                   lat_buf, kr_buf, fk_buf, fv_buf, lf_buf, sems,
                   m_m, l_m, acc_m, m_f, l_f, acc_f, carry, *, cp, nb, nch):
    b = pl.program_id(0)
    j = pl.program_id(1)
    total = nb * nch
    t = b * nch + j
    slot = t % DECODE_SLOTS
    kc = cp * PAGE_SIZE
    nt = (((1,), (1,)), ((), ()))
    bufs = (lat_buf, kr_buf, fk_buf, fv_buf, lf_buf)

    def start_chunk(step, sl):
        step = step % total
        bb = step // nch
        first_page = (nch - 1 - step % nch) * cp
        for p in range(cp):
            page = pt_ref[bb, first_page + p]
            dsts = (lat_buf.at[sl, p], kr_buf.at[sl, :, pl.ds(p * PAGE_SIZE, PAGE_SIZE)],
                    fk_buf.at[sl, p], fv_buf.at[sl, p], lf_buf.at[sl, p])
            for a, (hbm, dst) in enumerate(zip((lat_hbm, kr_hbm, fk_hbm, fv_hbm, lf_hbm), dsts)):
                pltpu.make_async_copy(hbm.at[page], dst, sems.at[a, sl]).start()

    def wait_chunk(sl):
        for a, buf in enumerate(bufs):
            pltpu.make_async_copy(buf.at[sl], buf.at[sl], sems.at[a, sl]).wait()

    @pl.when(t == 0)
    def _():
        for ahead in range(DECODE_SLOTS - 1):
            start_chunk(ahead, ahead)

    qlat = qlat_ref[...]
    qrope = qrope_ref[:, :ROPE_DIM]
    fq = fq_ref[...]
    head_col = lax.broadcasted_iota(jnp.int32, (FOX_HEADS, 1), 0)
    kv0 = head_col < FOX_GROUP

    @pl.when(j == 0)
    def _():
        ckvn = ckvn_ref[...].astype(BF16).astype(F32)
        krn = krn_ref[:, :ROPE_DIM].astype(BF16).astype(F32)
        m_m[...] = (jnp.sum(qlat.astype(F32) * ckvn, axis=-1, keepdims=True)
                    + jnp.sum(qrope.astype(F32) * krn, axis=-1, keepdims=True))
        l_m[...] = jnp.ones_like(l_m)
        acc_m[...] = jnp.broadcast_to(ckvn, acc_m.shape)
        fkn = fkn_ref[...].astype(BF16).astype(F32)
        fvn = fvn_ref[...].astype(BF16).astype(F32)
        fqf = fq.astype(F32)
        m_f[...] = jnp.where(kv0, jnp.sum(fqf * fkn[:, :HEAD_DIM], axis=-1, keepdims=True),
                             jnp.sum(fqf * fkn[:, HEAD_DIM:], axis=-1, keepdims=True))
        l_f[...] = jnp.ones_like(l_f)
        acc_f[...] = jnp.where(kv0, fvn[:, :HEAD_DIM], fvn[:, HEAD_DIM:])
        carry[...] = lfn_ref[...]

    wait_chunk(slot)
    start_chunk(t + DECODE_SLOTS - 1, (t + DECODE_SLOTS - 1) % DECODE_SLOTS)

    latb = lat_buf[slot].reshape(kc, KV_LORA).astype(BF16)
    krb = kr_buf[slot].astype(BF16)
    s = (lax.dot_general(qlat, latb, nt, preferred_element_type=F32)
         + jnp.dot(qrope, krb, preferred_element_type=F32))
    m_old = m_m[...]
    m_new = jnp.maximum(m_old, jnp.max(s, axis=-1, keepdims=True))
    alpha = jnp.exp(m_old - m_new)
    p = jnp.exp(s - m_new)
    l_m[...] = alpha * l_m[...] + jnp.sum(p, axis=-1, keepdims=True)
    acc_m[...] = alpha * acc_m[...] + jnp.dot(p.astype(BF16), latb, preferred_element_type=F32)
    m_m[...] = m_new

    lf = lf_buf[slot]
    x1, x2, x3 = _split3(lf.reshape(cp * FOX_HEADS, PAGE_SIZE))
    jr = lax.broadcasted_iota(jnp.int32, (PAGE_SIZE, PAGE_SIZE), 0)
    kcol = lax.broadcasted_iota(jnp.int32, (PAGE_SIZE, PAGE_SIZE), 1)
    later = (jr > kcol).astype(BF16)
    within = (jnp.dot(x1, later, preferred_element_type=F32)
              + jnp.dot(x2, later, preferred_element_type=F32)
              + jnp.dot(x3, later, preferred_element_type=F32))
    off = carry[...]
    tiles = [None] * cp
    for pg in range(cp - 1, -1, -1):
        tiles[pg] = within[pg * FOX_HEADS:(pg + 1) * FOX_HEADS, :] + off
        off = off + jnp.sum(lf[pg], axis=-1, keepdims=True)
    carry[...] = off
    bias = jnp.concatenate(tiles, axis=1)

    def kv_head(buf, n):
        rows = buf[slot, :, pl.ds(n, PAGE_SIZE, stride=FOX_KV_HEADS), :]
        return rows.reshape(kc, HEAD_DIM).astype(BF16)

    s = jnp.where(kv0, lax.dot_general(fq, kv_head(fk_buf, 0), nt, preferred_element_type=F32),
                  lax.dot_general(fq, kv_head(fk_buf, 1), nt, preferred_element_type=F32)) + bias
    m_old = m_f[...]
    m_new = jnp.maximum(m_old, jnp.max(s, axis=-1, keepdims=True))
    alpha = jnp.exp(m_old - m_new)
    p = jnp.exp(s - m_new)
    pb = p.astype(BF16)
    l_f[...] = alpha * l_f[...] + jnp.sum(p, axis=-1, keepdims=True)
    pv = jnp.where(kv0, jnp.dot(pb, kv_head(fv_buf, 0), preferred_element_type=F32),
                   jnp.dot(pb, kv_head(fv_buf, 1), preferred_element_type=F32))
    acc_f[...] = alpha * acc_f[...] + pv
    m_f[...] = m_new

    @pl.when(j == nch - 1)
    def _():
        olat_ref[...] = acc_m[...] / l_m[...]
        ofox_ref[...] = acc_f[...] / l_f[...]

    @pl.when(t == total - 1)
    def _():
        for ahead in range(1, DECODE_SLOTS):
            wait_chunk((t + ahead) % DECODE_SLOTS)


def _decode_attention(page_table, qlat, qrope, fq, ckv_new, kr_new, fk_new, fv_new, lf_new,
                      lat_c, kr_c, fk_c, fv_c, lf_c):
    nb, n_pages = page_table.shape
    cp = DECODE_PAGES
    ns = DECODE_SLOTS
    kvw = FOX_KV_HEADS * HEAD_DIM
    kv_rows = FOX_KV_HEADS * PAGE_SIZE
    per_b = lambda *shape: pl.BlockSpec((None,) + shape, lambda b, j, pt: (b, 0, 0))
    hbm = pl.BlockSpec(memory_space=pl.ANY)
    grid_spec = pltpu.PrefetchScalarGridSpec(
        num_scalar_prefetch=1,
        grid=(nb, n_pages // cp),
        in_specs=[per_b(MLA_HEADS, KV_LORA), per_b(MLA_HEADS, LANES), per_b(FOX_HEADS, HEAD_DIM),
                  per_b(1, KV_LORA), per_b(1, LANES), per_b(1, kvw), per_b(1, kvw),
                  per_b(FOX_HEADS, 1), hbm, hbm, hbm, hbm, hbm],
        out_specs=[per_b(MLA_HEADS, KV_LORA), per_b(FOX_HEADS, HEAD_DIM)],
        scratch_shapes=[
            pltpu.VMEM((ns, cp, PAGE_SIZE, KV_LORA), F32),
            pltpu.VMEM((ns, ROPE_DIM, cp * PAGE_SIZE), F32),
            pltpu.VMEM((ns, cp, kv_rows, HEAD_DIM), F32),
            pltpu.VMEM((ns, cp, kv_rows, HEAD_DIM), F32),
            pltpu.VMEM((ns, cp, FOX_HEADS, PAGE_SIZE), F32),
            pltpu.SemaphoreType.DMA((5, ns)),
            pltpu.VMEM((MLA_HEADS, 1), F32), pltpu.VMEM((MLA_HEADS, 1), F32),
            pltpu.VMEM((MLA_HEADS, KV_LORA), F32),
            pltpu.VMEM((FOX_HEADS, 1), F32), pltpu.VMEM((FOX_HEADS, 1), F32),
            pltpu.VMEM((FOX_HEADS, HEAD_DIM), F32),
            pltpu.VMEM((FOX_HEADS, 1), F32),
        ],
    )
    return pl.pallas_call(
        functools.partial(_decode_kernel, cp=cp, nb=nb, nch=n_pages // cp),
        grid_spec=grid_spec,
        out_shape=[jax.ShapeDtypeStruct((nb, MLA_HEADS, KV_LORA), F32),
                   jax.ShapeDtypeStruct((nb, FOX_HEADS, HEAD_DIM), F32)],
        compiler_params=_params("arbitrary", "arbitrary"),
        name="decode_attention",
    )(page_table, qlat, qrope, fq, ckv_new, kr_new, fk_new, fv_new, lf_new,
      lat_c, kr_c, fk_c, fv_c, lf_c)


def _swap_halves(x):
    half = x.shape[-1] // 2
    return jnp.concatenate([x[..., half:], x[..., :half]], axis=-1)


def _relayout_w_in(w):
    o = 0
    segs = {}
    for name, width in (("cq", Q_LORA), ("ckv", KV_LORA), ("kr", ROPE_DIM),
                        ("fq", FOX_HEADS * HEAD_DIM), ("fk", FOX_KV_HEADS * HEAD_DIM),
                        ("fv", FOX_KV_HEADS * HEAD_DIM), ("ff", FOX_HEADS)):
        segs[name] = w[:, o:o + width]
        o += width
    zeros = lambda n: jnp.zeros((w.shape[0], n), w.dtype)
    cols = [segs["cq"], segs["ckv"], segs["fq"], segs["fk"], segs["fv"],
            segs["kr"], _swap_halves(segs["kr"]), segs["ff"], zeros(LANES - FOX_HEADS),
            zeros(Z_WIDTH - Z_FF - LANES)]
    return jnp.concatenate(cols, axis=1).astype(BF16)


def _relayout_w_q_up(w):
    nope, rope = w[..., :HEAD_DIM], w[..., HEAD_DIM:]
    ext = jnp.concatenate([nope, rope, _swap_halves(rope)], axis=-1)
    return ext.reshape(w.shape[0], MLA_HEADS * QH).astype(BF16)


def _rope_tables(pos):
    half = ROPE_DIM // 2
    inv_freq = jnp.power(ROPE_THETA, -jnp.arange(half, dtype=F32) * 2.0 / ROPE_DIM)
    ang = pos.astype(F32)[:, None] * inv_freq[None, :]
    cos, sin = jnp.cos(ang), jnp.sin(ang)
    z = jnp.zeros((pos.shape[0], LANES - ROPE_DIM), F32)
    return (jnp.concatenate([cos, cos, z], axis=1), jnp.concatenate([-sin, sin, z], axis=1))


def _layer_front(x, mods, rows_per_batch, pos_tables, pos_blocks, tm, w, attend, extra_rows=0):
    sh1, sc1, ga1, sh2, sc2, _ = [_Mod(m, rows_per_batch) for m in mods]
    cs1, cs2 = pos_tables
    tm_small = min(tm, 256)
    h = _norm_mod(x, w["g_attn"], sc1, sh1, tm_small)
    z = _dense(h, w["w_in"], tm=tm, tn=512, out_dtype=F32, name="in_proj")
    (cqn, lat, kva, kr, fq, fk, fv, fkb, fvb, lf, lf128) = _post_in(
        z, w["g_q"], w["g_kv"], w["b_f"], cs1, cs2, tm_small, pos_blocks * (tm // tm_small) if pos_blocks > 1 else 1)
    pos_spec = pl.BlockSpec((tm, LANES), lambda i, j: (i % pos_blocks, 0))
    q256 = _dense(cqn, w["w_q"], tm=tm, tn=512, out_dtype=BF16, name="q_up",
                  epi=_epi_q_rope, extras=((cs1, pos_spec), (cs2, pos_spec)))
    ym, yf = attend(q256, kva, fq, fkb, fvb, lf128, lat, kr, fk, fv, lf)
    y = _norm2(ym, yf, w["g_mla"], w["g_fox"], tm_small)
    x = _dense(y, w["w_out"], tm=tm, tn=512, out_dtype=F32, name="out_proj", epi=_epi_residual,
               extras=((x, pl.BlockSpec((tm, 512), lambda i, j: (i, j))), (ga1.arr, ga1.spec(tm, 512))))
    h = _norm_mod(x, w["g_ffn"], sc2, sh2, math.gcd(tm_small, extra_rows) if extra_rows else tm_small,
                  extra_rows)
    return x, h, (lat, kr, fk, fv, lf)


def _ffn_down(a, first_row, x, ga2, rows_per_batch, w_down, tm):
    ga2 = _Mod(ga2, rows_per_batch)
    tn = 512
    return _dense(a, w_down, tm=tm, tn=tn, out_dtype=F32, name="ffn_down", epi=_epi_residual,
                  rows=x.shape[0], first_row=first_row,
                  extras=((x, pl.BlockSpec((tm, tn), lambda i, j: (i, j))), (ga2.arr, ga2.spec(tm, tn))))


def kernel(x_prompt, x_sample, c_prompt, c_sample, cache_mla_latent, cache_mla_krope, cache_fox_k,
           cache_fox_v, cache_fox_logf, page_table, w_ada, b_ada, g_attn_norm, w_in, g_q_norm, w_q_up,
           g_kv_norm, w_uk, w_uv, b_forget, g_mla_out, g_fox_out, w_out, g_ffn_norm, w_gate, w_up,
           w_down, g_final):
    nbp, seq, d = x_prompt.shape
    nbs = x_sample.shape[0]
    depth = w_ada.shape[0]
    n_phys = cache_mla_latent.shape[1]
    past = page_table.shape[1] * PAGE_SIZE
    kvw = FOX_KV_HEADS * HEAD_DIM

    xp = x_prompt.reshape(nbp * seq, d)
    xs = x_sample.reshape(nbs, d)
    c_all = jnp.concatenate([c_prompt, c_sample], axis=0)
    tables_p = _rope_tables(jnp.arange(seq, dtype=jnp.int32))
    tables_s = tuple(jnp.broadcast_to(t, (nbs, LANES))
                     for t in _rope_tables(past + jnp.arange(1, dtype=jnp.int32)))
    tm_p = 1024
    st_p, st_s = [], []
    for l in range(depth):
        w = dict(g_attn=g_attn_norm[l], w_in=_relayout_w_in(w_in[l]), g_q=g_q_norm[l],
                 w_q=_relayout_w_q_up(w_q_up[l]), g_kv=g_kv_norm[l], b_f=b_forget[l],
                 g_mla=g_mla_out[l], g_fox=g_fox_out[l], w_out=w_out[l], g_ffn=g_ffn_norm[l],
                 w_gate=w_gate[l], w_up=w_up[l], w_down=w_down[l].astype(BF16))
        w_uk2d = w_uk[l].reshape(KV_LORA, MLA_HEADS * HEAD_DIM)
        w_uv2d = w_uv[l].reshape(KV_LORA, MLA_HEADS * HEAD_DIM)
        mod = _ada(c_all, w_ada[l], b_ada[l])
        mods_p = [mod[:nbp, i * d:(i + 1) * d] for i in range(N_MOD)]
        mods_s = [mod[nbp:, i * d:(i + 1) * d] for i in range(N_MOD)]

        def attend_prompt(q256, kva, fq, fkb, fvb, lf128, *_):
            k256, v = _kv_up(kva, w_uk2d, w_uv2d, tm_p)
            ym = _flash_mla(q256.reshape(nbp, seq, -1), k256.reshape(nbp, seq, -1),
                            v.reshape(nbp, seq, -1), FLASH_BLOCK)
            e, et = _suffix(lf128.reshape(nbp, seq, LANES))
            yf = _flash_fox(fq.reshape(nbp, seq, -1), fkb.reshape(nbp, seq, kvw),
                            fvb.reshape(nbp, seq, kvw), e, et, FLASH_BLOCK)
            return ym.reshape(nbp * seq, -1), yf.reshape(nbp * seq, -1)

        def attend_sample(q256, kva, fq, fkb, fvb, lf128, lat, kr, fk, fv, lf):
            qlat = _absorb(q256, w_uk2d).reshape(nbs, MLA_HEADS, KV_LORA)
            qrope = q256.reshape(nbs, MLA_HEADS, QH)[:, :, LANES:]
            o_lat, o_fox = _decode_attention(
                page_table, qlat, qrope, fq.reshape(nbs, FOX_HEADS, HEAD_DIM),
                lat.reshape(nbs, 1, KV_LORA), kva[:, KV_LORA:].astype(F32).reshape(nbs, 1, LANES),
                fk.reshape(nbs, 1, kvw), fv.reshape(nbs, 1, kvw), lf.reshape(nbs, FOX_HEADS, 1),
                cache_mla_latent[l], jnp.swapaxes(cache_mla_krope[l], 1, 2),
                cache_fox_k[l].reshape(n_phys, PAGE_SIZE * FOX_KV_HEADS, HEAD_DIM),
                cache_fox_v[l].reshape(n_phys, PAGE_SIZE * FOX_KV_HEADS, HEAD_DIM),
                jnp.swapaxes(cache_fox_logf[l], 1, 2))
            ym = _value_up(o_lat.reshape(nbs, MLA_HEADS * KV_LORA), w_uv2d)
            return ym, o_fox.reshape(nbs, FOX_HEADS * HEAD_DIM)

        xs, hs, ss = _layer_front(xs, mods_s, 1, tables_s, 1, nbs, w, attend_sample)
        xp, hp, sp = _layer_front(xp, mods_p, seq, tables_p, seq // tm_p, tm_p, w, attend_prompt,
                                  extra_rows=nbs)
        n_p = nbp * seq
        h_all = _place_rows(hp, hs)
        a_all = _gate_up(h_all, w["w_gate"], w["w_up"], (n_p + nbs) // GATE_UP_ROW_TILES, 256)
        xp = _ffn_down(a_all, 0, xp, mods_p[5], seq, w["w_down"], 512)
        xs = _ffn_down(a_all, n_p, xs, mods_s[5], 1, w["w_down"], nbs)
        st_p.append(sp)
        st_s.append(ss)

    y_prompt = _norm(xp, g_final, 256).reshape(nbp, seq, d)
    y_sample = _norm(xs, g_final, nbs).reshape(nbs, 1, d)

    def stack(states, idx, shape):
        return jnp.stack([s[idx].reshape(shape) for s in states])

    outs = [y_prompt, y_sample]
    for states, lead in ((st_p, (nbp, seq)), (st_s, (nbs, 1))):
        outs += [stack(states, 0, lead + (KV_LORA,)), stack(states, 1, lead + (ROPE_DIM,)),
                 stack(states, 2, lead + (FOX_KV_HEADS, HEAD_DIM)),
                 stack(states, 3, lead + (FOX_KV_HEADS, HEAD_DIM)),
                 stack(states, 4, lead + (FOX_HEADS,))]
    return tuple(outs)
```
